```python
import jax, jax.numpy as jnp
from jax import lax
import numpy as np

D_MODEL = 1024
BATCH = 1
SEQ = 16384
DEPTH = 4

CHUNK = 64
N_MIXERS = 2
N_A_LAYERS = (DEPTH + 1) // 2
N_B_LAYERS = DEPTH // 2
PLE_DIM = 256
FFN_HIDDEN = 2816
ML_HEADS = 4
ML_DQK = D_MODEL // 4
ML_DV = D_MODEL // 2
ML_INNER = ML_HEADS * ML_DV
ML_QK = ML_HEADS * ML_DQK
ML_CONV = 4
ML_IN_COLS = 2 * ML_QK + 2 * ML_INNER + 2 * ML_HEADS
GM_BLOCK = 128
GM_GROUPS = 8
GM_WIDTH = 2 * D_MODEL
GM_GDIM = GM_WIDTH // GM_GROUPS
DN_ALPHA = (2.0 * DEPTH) ** 0.25
DN_BETA = (8.0 * DEPTH) ** -0.25
LN_EPS = 1e-5

kernel_name = 'hybrid_mlstm_gmlp_macaron_deepnorm'


def layer_norm(x, g, b):
    x32 = x.astype(jnp.float32)
    mu = jnp.mean(x32, axis=-1, keepdims=True)
    var = jnp.mean(jnp.square(x32 - mu), axis=-1, keepdims=True)
    y = (x32 - mu) * lax.rsqrt(var + LN_EPS) * g.astype(jnp.float32) + b.astype(jnp.float32)
    return y.astype(x.dtype)


def swiglu_ffn(x, w_gu, w_d):
    g, u = jnp.split(x @ w_gu, 2, axis=-1)
    return (jax.nn.silu(g) * u) @ w_d


def causal_depthwise_conv(x, w):
    return lax.conv_general_dilated(
        x, w[:, None, :], window_strides=(1,), padding=[(w.shape[0] - 1, 0)],
        dimension_numbers=('NWC', 'WIO', 'NWC'), feature_group_count=x.shape[-1])


def mlstm_cell(q, k, v, ig, fg):
    B, S, H, DK = q.shape
    DV = v.shape[-1]
    NC = S // CHUNK

    def to_chunks(a):
        return a.reshape(B, NC, CHUNK, H, a.shape[-1]).transpose(1, 0, 3, 2, 4)

    def gate_chunks(a):
        return a.reshape(B, NC, CHUNK, H).transpose(1, 0, 3, 2)

    qc, kc, vc = to_chunks(q), to_chunks(k), to_chunks(v)
    igc = gate_chunks(ig)
    bc = jnp.cumsum(gate_chunks(jax.nn.log_sigmoid(fg)), axis=-1)
    mask = jnp.tril(jnp.ones((CHUNK, CHUNK), dtype=bool))

    def step(carry, xs):
        C, n, m = carry
        q_, k_, v_, i_, b_ = xs
        d_log = jnp.where(mask, b_[..., :, None] - b_[..., None, :] + i_[..., None, :], -jnp.inf)
        inter = b_ + m[..., None]
        m_t = jnp.maximum(inter, jnp.max(d_log, axis=-1))
        s_mat = jnp.einsum('bhtd,bhsd->bhts', q_, k_) * jnp.exp(d_log - m_t[..., None])
        w_inter = jnp.exp(inter - m_t)
        num = (jnp.einsum('bhts,bhsv->bhtv', s_mat, v_)
               + w_inter[..., None] * jnp.einsum('bhvd,bhtd->bhtv', C, q_))
        den = jnp.sum(s_mat, axis=-1) + w_inter * jnp.einsum('bhd,bhtd->bht', n, q_)
        h = num / jnp.maximum(jnp.abs(den), jnp.exp(-m_t))[..., None]
        b_last = b_[..., -1]
        w_log = b_last[..., None] - b_ + i_
        m_new = jnp.maximum(b_last + m, jnp.max(w_log, axis=-1))
        decay = jnp.exp(b_last + m - m_new)
        w_s = jnp.exp(w_log - m_new[..., None])
        C = decay[..., None, None] * C + jnp.einsum('bhs,bhsv,bhsd->bhvd', w_s, v_, k_)
        n = decay[..., None] * n + jnp.einsum('bhs,bhsd->bhd', w_s, k_)
        return (C, n, m_new), h

    init = (jnp.zeros((B, H, DV, DK), jnp.float32), jnp.zeros((B, H, DK), jnp.float32),
            jnp.zeros((B, H), jnp.float32))
    _, h = lax.scan(step, init, (qc, kc, vc, igc, bc))
    return h.transpose(1, 0, 3, 2, 4).reshape(B, S, H, DV)


def mlstm_mixer(x, w_in, b_in, conv_w, norm_g, w_out):
    B, S, _ = x.shape
    proj = x @ w_in + b_in
    qk, v, z, gates = jnp.split(proj, [2 * ML_QK, 2 * ML_QK + ML_INNER, 2 * ML_QK + 2 * ML_INNER], axis=-1)
    qk = jax.nn.silu(causal_depthwise_conv(qk, conv_w))
    q, k = jnp.split(qk, 2, axis=-1)
    q = q.reshape(B, S, ML_HEADS, ML_DQK).astype(jnp.float32)
    k = (k.reshape(B, S, ML_HEADS, ML_DQK) * (ML_DQK ** -0.5)).astype(jnp.float32)
    v = v.reshape(B, S, ML_HEADS, ML_DV).astype(jnp.float32)
    ig, fg = jnp.split(gates.astype(jnp.float32), 2, axis=-1)
    h = mlstm_cell(q, k, v, ig, fg)
    mu = jnp.mean(h, axis=-1, keepdims=True)
    var = jnp.mean(jnp.square(h - mu), axis=-1, keepdims=True)
    h = ((h - mu) * lax.rsqrt(var + LN_EPS)).reshape(B, S, ML_INNER) * norm_g.astype(jnp.float32)
    h = (h * jax.nn.sigmoid(z.astype(jnp.float32))).astype(x.dtype)
    return h @ w_out


def gmlp_mixer(x, w_in, b_in, vn_g, vn_b, ws, bs, w_out):
    B, S, _ = x.shape
    NB = S // GM_BLOCK
    u, v = jnp.split(jax.nn.gelu(x @ w_in + b_in, approximate=False), 2, axis=-1)
    v = layer_norm(v, vn_g, vn_b).reshape(B, NB, GM_BLOCK, GM_GROUPS, GM_GDIM)
    blk = jnp.arange(GM_BLOCK) // CHUNK
    mask = blk[:, None] >= blk[None, :]
    ws_m = jnp.where(mask, ws, jnp.zeros((), ws.dtype))
    s = jnp.einsum('gts,bnsgc->bntgc', ws_m, v) + bs.T[:, :, None]
    return (u * s.reshape(B, S, GM_WIDTH)) @ w_out


def setup_inputs(seed: int = 0) -> dict:
    key = jax.random.key(seed)
    ks = jax.random.split(key, 32)
    nrm = lambda k, shape: jax.random.normal(k, shape, jnp.float32)
    D, F = D_MODEL, FFN_HIDDEN
    ml_b_in = 0.02 * nrm(ks[8], (N_A_LAYERS, ML_IN_COLS))
    ml_b_in = ml_b_in.at[:, -ML_HEADS:].add(jnp.linspace(3.0, 6.0, ML_HEADS, dtype=jnp.float32))
    return {
        'x': nrm(ks[0], (BATCH, SEQ, D)),
        'p': nrm(ks[1], (DEPTH, BATCH, SEQ, PLE_DIM)),
        'ln_g': 1.0 + 0.02 * nrm(ks[2], (DEPTH, 4, D)),
        'ln_b': 0.02 * nrm(ks[3], (DEPTH, 4, D)),
        'ffn1_wgu': nrm(ks[4], (DEPTH, D, 2 * F)) * D ** -0.5,
        'ffn1_wd': nrm(ks[5], (DEPTH, F, D)) * (F ** -0.5 * DN_BETA),
        'ffn2_wgu': nrm(ks[6], (DEPTH, D, 2 * F)) * D ** -0.5,
        'ffn2_wd': nrm(ks[7], (DEPTH, F, D)) * (F ** -0.5 * DN_BETA),
        'ml_w_in': nrm(ks[9], (N_A_LAYERS, D, ML_IN_COLS)) * D ** -0.5,
        'ml_b_in': ml_b_in,
        'ml_conv': nrm(ks[10], (N_A_LAYERS, ML_CONV, 2 * ML_QK)) * ML_CONV ** -0.5,
        'ml_norm_g': 1.0 + 0.02 * nrm(ks[11], (N_A_LAYERS, ML_INNER)),
        'ml_w_out': nrm(ks[12], (N_A_LAYERS, ML_INNER, D)) * (ML_INNER ** -0.5 * DN_BETA),
        'gm_w_in': nrm(ks[13], (N_B_LAYERS, D, 2 * GM_WIDTH)) * D ** -0.5,
        'gm_b_in': 0.02 * nrm(ks[14], (N_B_LAYERS, 2 * GM_WIDTH)),
        'gm_vn_g': 1.0 + 0.02 * nrm(ks[15], (N_B_LAYERS, GM_WIDTH)),
        'gm_vn_b': 0.02 * nrm(ks[16], (N_B_LAYERS, GM_WIDTH)),
        'gm_ws': nrm(ks[17], (N_B_LAYERS, GM_GROUPS, GM_BLOCK, GM_BLOCK)) * GM_BLOCK ** -0.5,
        'gm_bs': 1.0 + 0.02 * nrm(ks[18], (N_B_LAYERS, GM_GROUPS, GM_BLOCK)),
        'gm_w_out': nrm(ks[19], (N_B_LAYERS, GM_WIDTH, D)) * (GM_WIDTH ** -0.5 * DN_BETA),
        'ple_wp': nrm(ks[20], (DEPTH, PLE_DIM, D)) * (PLE_DIM ** -0.5 * DN_BETA),
        'ple_wg': nrm(ks[21], (DEPTH, D, D)) * D ** -0.5,
        'ple_bg': 0.02 * nrm(ks[22], (DEPTH, D)),
    }


def reference(x, p, ln_g, ln_b, ffn1_wgu, ffn1_wd, ffn2_wgu, ffn2_wd,
              ml_w_in, ml_b_in, ml_conv, ml_norm_g, ml_w_out,
              gm_w_in, gm_b_in, gm_vn_g, gm_vn_b, gm_ws, gm_bs, gm_w_out,
              ple_wp, ple_wg, ple_bg):
    for i in range(DEPTH):
        x = layer_norm(DN_ALPHA * x + 0.5 * swiglu_ffn(x, ffn1_wgu[i], ffn1_wd[i]), ln_g[i, 0], ln_b[i, 0])
        j = i // N_MIXERS
        if i % N_MIXERS == 0:
            mix = mlstm_mixer(x, ml_w_in[j], ml_b_in[j], ml_conv[j], ml_norm_g[j], ml_w_out[j])
        else:
            mix = gmlp_mixer(x, gm_w_in[j], gm_b_in[j], gm_vn_g[j], gm_vn_b[j], gm_ws[j], gm_bs[j], gm_w_out[j])
        x = layer_norm(DN_ALPHA * x + mix, ln_g[i, 1], ln_b[i, 1])
        x = layer_norm(DN_ALPHA * x + 0.5 * swiglu_ffn(x, ffn2_wgu[i], ffn2_wd[i]), ln_g[i, 2], ln_b[i, 2])
        ple = jax.nn.sigmoid(x @ ple_wg[i] + ple_bg[i]) * (p[i] @ ple_wp[i])
        x = layer_norm(DN_ALPHA * x + ple, ln_g[i, 3], ln_b[i, 3])
    return x
```

```python
import functools

import jax
import jax.numpy as jnp
from jax import lax
from jax.experimental import pallas as pl
from jax.experimental.pallas import tpu as pltpu

D_MODEL = 1024
SEQ = 16384
DEPTH = 4
CHUNK = 64
PLE_DIM = 256
FFN_HIDDEN = 2816
ML_HEADS = 4
ML_DQK = 256
ML_DV = 512
ML_INNER = ML_HEADS * ML_DV
ML_QK = ML_HEADS * ML_DQK
ML_CONV = 4
GM_BLOCK = 128
GM_GROUPS = 8
GM_WIDTH = 2 * D_MODEL
GM_GDIM = GM_WIDTH // GM_GROUPS
DN_ALPHA = (2.0 * DEPTH) ** 0.25
LN_EPS = 1e-5

V7X_MXU_DIM = 256
V7X_SUBLANES = 8
V7X_VMEM_BYTES = 64 * 1024 * 1024
VMEM_LIMIT_BYTES = V7X_VMEM_BYTES * 7 // 8

FFN_ROWS = 512
FFN_COLS = V7X_MXU_DIM
ML_ROWS = 256
GM_ROWS = 256

F32 = jnp.float32
BF16 = jnp.bfloat16


def _dot(a, b):
    return jnp.dot(a, b, preferred_element_type=F32)


def _dot_nt(a, b):
    return lax.dot_general(a, b, (((1,), (1,)), ((), ())), preferred_element_type=F32)


def _dot_tn(a, b):
    return lax.dot_general(a, b, (((0,), (0,)), ((), ())), preferred_element_type=F32)


def _layer_norm(z, g, b):
    mu = jnp.mean(z, axis=-1, keepdims=True)
    zc = z - mu
    var = jnp.mean(zc * zc, axis=-1, keepdims=True)
    return zc * lax.rsqrt(var + LN_EPS) * g + b


def _split3(a):
    hi = a.astype(BF16)
    r1 = a - hi.astype(F32)
    mid = r1.astype(BF16)
    lo = (r1 - mid.astype(F32)).astype(BF16)
    return hi, mid, lo


def _gelu(x):
    return 0.5 * x * (1.0 + lax.erf(x * (2.0 ** -0.5)))


def _log_sigmoid(x):
    return jnp.minimum(x, 0.0) - jnp.log1p(jnp.exp(-jnp.abs(x)))


def _ffn_kernel(with_ple, ln_row, *refs):
    if with_ple:
        (x_ref, wgu_ref, wd_ref, lng_ref, lnb_ref, p_ref, wpg_ref, bpg_ref, wpp_ref, o_ref, a_ref) = refs
    else:
        (x_ref, wgu_ref, wd_ref, lng_ref, lnb_ref, o_ref, a_ref) = refs
    x = x_ref[...]
    xb = x.astype(BF16)
    for c in range(FFN_HIDDEN // FFN_COLS):
        lo = c * FFN_COLS
        g = _dot(xb, wgu_ref[:, lo:lo + FFN_COLS])
        u = _dot(xb, wgu_ref[:, FFN_HIDDEN + lo:FFN_HIDDEN + lo + FFN_COLS])
        a_ref[:, lo:lo + FFN_COLS] = (jax.nn.silu(g) * u).astype(BF16)
    y = _dot(a_ref[...], wd_ref[...])
    x1 = _layer_norm(DN_ALPHA * x + 0.5 * y, lng_ref[ln_row:ln_row + 1, :], lnb_ref[ln_row:ln_row + 1, :])
    if with_ple:
        gate = jax.nn.sigmoid(_dot(x1.astype(BF16), wpg_ref[...]) + bpg_ref[...])
        emb = _dot(p_ref[...].astype(BF16), wpp_ref[...])
        x1 = _layer_norm(DN_ALPHA * x1 + gate * emb, lng_ref[3:4, :], lnb_ref[3:4, :])
    o_ref[...] = x1


def _resident(shape, index):
    return pl.BlockSpec(shape, lambda t: index, pipeline_mode=pl.Buffered(1))


def _ffn_call(x, wgu, wd, ln_g, ln_b, layer, ln_row, ple=None):
    S, D = x.shape
    F = FFN_HIDDEN
    rows = pl.BlockSpec((FFN_ROWS, D), lambda t: (t, 0))
    in_specs = [
        rows,
        _resident((None, D, 2 * F), (layer, 0, 0)),
        _resident((None, F, D), (layer, 0, 0)),
        _resident((None, 4, D), (layer, 0, 0)),
        _resident((None, 4, D), (layer, 0, 0)),
    ]
    args = [x, wgu, wd, ln_g, ln_b]
    if ple is not None:
        p, wpg, bpg, wpp = ple
        in_specs += [
            pl.BlockSpec((None, FFN_ROWS, PLE_DIM), lambda t: (layer, t, 0)),
            _resident((None, D, D), (layer, 0, 0)),
            _resident((None, 1, D), (layer, 0, 0)),
            _resident((None, PLE_DIM, D), (layer, 0, 0)),
        ]
        args += [p, wpg, bpg, wpp]
    return pl.pallas_call(
        functools.partial(_ffn_kernel, ple is not None, ln_row),
        grid=(S // FFN_ROWS,),
        in_specs=in_specs,
        out_specs=rows,
        out_shape=jax.ShapeDtypeStruct((S, D), F32),
        scratch_shapes=[pltpu.VMEM((FFN_ROWS, F), BF16)],
        compiler_params=pltpu.CompilerParams(
            dimension_semantics=("parallel",), vmem_limit_bytes=VMEM_LIMIT_BYTES),
        name="ffn_ple" if ple is not None else "ffn",
    )(*args)


def _mlstm_kernel(x_ref, win_ref, bin_ref, wgc_ref, wgr_ref, bgc_ref, bgr_ref, conv_ref, ng_ref,
                  wout_ref, lng_ref, lnb_ref, o_ref,
                  qk_ref, ct_ref, n_ref, m_ref, hb_ref):
    L = ML_ROWS
    HALO = V7X_SUBLANES

    @pl.when(pl.program_id(0) == 0)
    def _init():
        qk_ref[0:HALO, :] = jnp.zeros((HALO, 2 * ML_QK), F32)
        ct_ref[...] = jnp.zeros_like(ct_ref)
        n_ref[...] = jnp.zeros_like(n_ref)
        m_ref[...] = jnp.zeros_like(m_ref)

    x = x_ref[...]
    xb = x.astype(BF16)

    qk_ref[HALO:HALO + L, :] = _dot(xb, win_ref[:, 0:2 * ML_QK]) + bin_ref[:, 0:2 * ML_QK]

    g_col = _dot(xb, wgc_ref[...]) + bgc_ref[...]
    g_row = _dot_nt(wgr_ref[...], xb) + bgr_ref[...]
    ri = lax.broadcasted_iota(jnp.int32, (L, L), 0)
    ci = lax.broadcasted_iota(jnp.int32, (L, L), 1)
    causal = ri >= ci
    lower = causal.astype(BF16)
    upper = (ri <= ci).astype(BF16)
    fh, fm, fl = _split3(_log_sigmoid(g_col))
    bcum_col = _dot(lower, fh) + _dot(lower, fm) + _dot(lower, fl)
    fh, fm, fl = _split3(_log_sigmoid(g_row))
    bcum_row = _dot(fh, upper) + _dot(fm, upper) + _dot(fl, upper)

    for h in range(ML_HEADS):
        qlo = h * ML_DQK
        klo = ML_QK + h * ML_DQK
        q = jnp.zeros((L, ML_DQK), F32)
        k = jnp.zeros((L, ML_DQK), F32)
        for j in range(ML_CONV):
            rows = pl.ds(HALO - (ML_CONV - 1) + j, L)
            q = q + conv_ref[j:j + 1, qlo:qlo + ML_DQK] * qk_ref[rows, qlo:qlo + ML_DQK]
            k = k + conv_ref[j:j + 1, klo:klo + ML_DQK] * qk_ref[rows, klo:klo + ML_DQK]
        q = jax.nn.silu(q)
        k = jax.nn.silu(k) * (ML_DQK ** -0.5)
        qb = q.astype(BF16)
        kb = k.astype(BF16)
        vlo = 2 * ML_QK + h * ML_DV
        v = _dot(xb, win_ref[:, vlo:vlo + ML_DV]) + bin_ref[:, vlo:vlo + ML_DV]
        vb = v.astype(BF16)

        i_col = g_col[:, h:h + 1]
        b_col = bcum_col[:, ML_HEADS + h:ML_HEADS + h + 1]
        i_row = g_row[h:h + 1, :]
        b_row = bcum_row[ML_HEADS + h:ML_HEADS + h + 1, :]
        m_prev = m_ref[h][:, 0:1]
        ct = ct_ref[h]
        n_prev = n_ref[h]

        d_log = jnp.where(causal, b_col - b_row + i_row, -jnp.inf)
        inter = b_col + m_prev
        m_t = jnp.maximum(inter, jnp.max(d_log, axis=-1, keepdims=True))
        s_mat = _dot_nt(qb, kb) * jnp.exp(d_log - m_t)
        w_inter = jnp.exp(inter - m_t)
        num = _dot(s_mat.astype(BF16), vb) + w_inter * _dot(qb, ct.astype(BF16))
        den = jnp.sum(s_mat, axis=-1, keepdims=True) + w_inter * jnp.sum(q * n_prev, axis=-1, keepdims=True)
        hh = num / jnp.maximum(jnp.abs(den), jnp.exp(-m_t))

        b_last = b_col[L - 1:L, :]
        w_log = b_last - b_col + i_col
        m_new = jnp.maximum(b_last + m_prev, jnp.max(w_log, axis=0, keepdims=True))
        decay = jnp.exp(b_last + m_prev - m_new)
        kw = k * jnp.exp(w_log - m_new)
        ct_ref[h] = decay * ct + _dot_tn(kw.astype(BF16), vb)
        n_ref[h] = decay * n_prev + jnp.sum(kw, axis=0, keepdims=True)
        m_ref[h] = jnp.broadcast_to(m_new, m_ref.shape[1:])

        mu = jnp.mean(hh, axis=-1, keepdims=True)
        hc = hh - mu
        var = jnp.mean(hc * hc, axis=-1, keepdims=True)
        hn = hc * lax.rsqrt(var + LN_EPS) * ng_ref[:, h * ML_DV:(h + 1) * ML_DV]
        zlo = 2 * ML_QK + ML_INNER + h * ML_DV
        z = _dot(xb, win_ref[:, zlo:zlo + ML_DV]) + bin_ref[:, zlo:zlo + ML_DV]
        hb_ref[:, h * ML_DV:(h + 1) * ML_DV] = (hn * jax.nn.sigmoid(z)).astype(BF16)

    qk_ref[0:HALO, :] = qk_ref[L:L + HALO, :]

    y = _dot(hb_ref[...], wout_ref[...])
    o_ref[...] = _layer_norm(DN_ALPHA * x + y, lng_ref[1:2, :], lnb_ref[1:2, :])


def _mlstm_call(x, w_in, b_in, wg_col, wg_row, bg_col, bg_row, conv_w, norm_g, w_out, ln_g, ln_b, layer, j):
    S, D = x.shape
    L = ML_ROWS
    NM = 2 * ML_QK + 2 * ML_INNER
    rows = pl.BlockSpec((L, D), lambda t: (t, 0))
    in_specs = [
        rows,
        _resident((None, D, NM), (j, 0, 0)),
        _resident((None, 1, NM), (j, 0, 0)),
        _resident((None, D, 2 * ML_HEADS), (j, 0, 0)),
        _resident((None, 4 * ML_HEADS, D), (j, 0, 0)),
        _resident((None, 1, 2 * ML_HEADS), (j, 0, 0)),
        _resident((None, 4 * ML_HEADS, 1), (j, 0, 0)),
        _resident((None, ML_CONV, 2 * ML_QK), (j, 0, 0)),
        _resident((None, 1, ML_INNER), (j, 0, 0)),
        _resident((None, ML_INNER, D), (j, 0, 0)),
        _resident((None, 4, D), (layer, 0, 0)),
        _resident((None, 4, D), (layer, 0, 0)),
    ]
    return pl.pallas_call(
        _mlstm_kernel,
        grid=(S // L,),
        in_specs=in_specs,
        out_specs=rows,
        out_shape=jax.ShapeDtypeStruct((S, D), F32),
        scratch_shapes=[
            pltpu.VMEM((V7X_SUBLANES + L, 2 * ML_QK), F32),
            pltpu.VMEM((ML_HEADS, ML_DQK, ML_DV), F32),
            pltpu.VMEM((ML_HEADS, 1, ML_DQK), F32),
            pltpu.VMEM((ML_HEADS, 1, 128), F32),
            pltpu.VMEM((L, ML_INNER), BF16),
        ],
        compiler_params=pltpu.CompilerParams(
            dimension_semantics=("arbitrary",), vmem_limit_bytes=VMEM_LIMIT_BYTES),
        name="mlstm",
    )(x, w_in, b_in, wg_col, wg_row, bg_col, bg_row, conv_w, norm_g, w_out, ln_g, ln_b)


def _gmlp_kernel(x_ref, win_ref, bin_ref, vng_ref, vnb_ref, ws_ref, bst_ref, wout_ref, lng_ref, lnb_ref,
                 o_ref, vb_ref, y_ref):
    x = x_ref[...]
    xb = x.astype(BF16)
    v = _gelu(_dot(xb, win_ref[:, GM_WIDTH:2 * GM_WIDTH]) + bin_ref[:, GM_WIDTH:2 * GM_WIDTH])
    vb_ref[...] = _layer_norm(v, vng_ref[...], vnb_ref[...]).astype(BF16)

    blk_r = lax.broadcasted_iota(jnp.int32, (GM_BLOCK, GM_BLOCK), 0) // CHUNK
    blk_c = lax.broadcasted_iota(jnp.int32, (GM_BLOCK, GM_BLOCK), 1) // CHUNK
    chunk_causal = blk_r >= blk_c
    for g in range(GM_GROUPS):
        lo = g * GM_GDIM
        u = _gelu(_dot(xb, win_ref[:, lo:lo + GM_GDIM]) + bin_ref[:, lo:lo + GM_GDIM])
        wsm = jnp.where(chunk_causal, ws_ref[g], 0.0).astype(BF16)
        bias = bst_ref[:, g:g + 1]
        for nb in range(GM_ROWS // GM_BLOCK):
            r0 = nb * GM_BLOCK
            s = _dot(wsm, vb_ref[r0:r0 + GM_BLOCK, lo:lo + GM_GDIM]) + bias
            y_ref[r0:r0 + GM_BLOCK, lo:lo + GM_GDIM] = (u[r0:r0 + GM_BLOCK, :] * s).astype(BF16)
    y = _dot(y_ref[...], wout_ref[...])
    o_ref[...] = _layer_norm(DN_ALPHA * x + y, lng_ref[1:2, :], lnb_ref[1:2, :])


def _gmlp_call(x, w_in, b_in, vn_g, vn_b, ws, bs_t, w_out, ln_g, ln_b, layer, j):
    S, D = x.shape
    rows = pl.BlockSpec((GM_ROWS, D), lambda t: (t, 0))
    in_specs = [
        rows,
        _resident((None, D, 2 * GM_WIDTH), (j, 0, 0)),
        _resident((None, 1, 2 * GM_WIDTH), (j, 0, 0)),
        _resident((None, 1, GM_WIDTH), (j, 0, 0)),
        _resident((None, 1, GM_WIDTH), (j, 0, 0)),
        _resident((None, GM_GROUPS, GM_BLOCK, GM_BLOCK), (j, 0, 0, 0)),
        _resident((None, GM_BLOCK, GM_GROUPS), (j, 0, 0)),
        _resident((None, GM_WIDTH, D), (j, 0, 0)),
        _resident((None, 4, D), (layer, 0, 0)),
        _resident((None, 4, D), (layer, 0, 0)),
    ]
    return pl.pallas_call(
        _gmlp_kernel,
        grid=(S // GM_ROWS,),
        in_specs=in_specs,
        out_specs=rows,
        out_shape=jax.ShapeDtypeStruct((S, D), F32),
        scratch_shapes=[pltpu.VMEM((GM_ROWS, GM_WIDTH), BF16), pltpu.VMEM((GM_ROWS, GM_WIDTH), BF16)],
        compiler_params=pltpu.CompilerParams(
            dimension_semantics=("parallel",), vmem_limit_bytes=VMEM_LIMIT_BYTES),
        name="gmlp",
    )(x, w_in, b_in, vn_g, vn_b, ws, bs_t, w_out, ln_g, ln_b)


def kernel(x, p, ln_g, ln_b, ffn1_wgu, ffn1_wd, ffn2_wgu, ffn2_wd, ml_w_in, ml_b_in, ml_conv, ml_norm_g, ml_w_out, gm_w_in, gm_b_in, gm_vn_g, gm_vn_b, gm_ws, gm_bs, gm_w_out, ple_wp, ple_wg, ple_bg):
    B, S, D = x.shape
    assert (B, S, D) == (1, SEQ, D_MODEL)
    xs = x.reshape(S, D)
    ps = p.reshape(DEPTH, S, PLE_DIM)

    ffn1_wgu_b, ffn1_wd_b = ffn1_wgu.astype(BF16), ffn1_wd.astype(BF16)
    ffn2_wgu_b, ffn2_wd_b = ffn2_wgu.astype(BF16), ffn2_wd.astype(BF16)
    ple_wg_b, ple_wp_b = ple_wg.astype(BF16), ple_wp.astype(BF16)
    ple_bg_r = ple_bg[:, None, :]

    NM = 2 * ML_QK + 2 * ML_INNER
    ml_w_main = ml_w_in[:, :, :NM].astype(BF16)
    ml_b_main = ml_b_in[:, None, :NM]
    ml_wg_col = ml_w_in[:, :, NM:].astype(BF16)
    ml_wg_row = jnp.pad(jnp.swapaxes(ml_w_in[:, :, NM:], 1, 2),
                        ((0, 0), (0, 2 * ML_HEADS), (0, 0))).astype(BF16)
    ml_bg_col = ml_b_in[:, None, NM:]
    ml_bg_row = jnp.pad(ml_b_in[:, NM:], ((0, 0), (0, 2 * ML_HEADS)))[:, :, None]
    ml_norm_g_r = ml_norm_g[:, None, :]
    ml_w_out_b = ml_w_out.astype(BF16)

    gm_w_in_b, gm_w_out_b = gm_w_in.astype(BF16), gm_w_out.astype(BF16)
    gm_b_in_r = gm_b_in[:, None, :]
    gm_vn_g_r, gm_vn_b_r = gm_vn_g[:, None, :], gm_vn_b[:, None, :]
    gm_bs_t = jnp.swapaxes(gm_bs, 1, 2)

    for i in range(DEPTH):
        xs = _ffn_call(xs, ffn1_wgu_b, ffn1_wd_b, ln_g, ln_b, i, 0)
        j = i // 2
        if i % 2 == 0:
            xs = _mlstm_call(xs, ml_w_main, ml_b_main, ml_wg_col, ml_wg_row, ml_bg_col, ml_bg_row,
                             ml_conv, ml_norm_g_r, ml_w_out_b, ln_g, ln_b, i, j)
        else:
            xs = _gmlp_call(xs, gm_w_in_b, gm_b_in_r, gm_vn_g_r, gm_vn_b_r, gm_ws, gm_bs_t,
                            gm_w_out_b, ln_g, ln_b, i, j)
        xs = _ffn_call(xs, ffn2_wgu_b, ffn2_wd_b, ln_g, ln_b, i, 2,
                       ple=(ps, ple_wg_b, ple_bg_r, ple_wp_b))
    return xs.reshape(B, S, D)
```

```python
import functools

import jax
import jax.numpy as jnp
from jax import lax
from jax.experimental import pallas as pl
from jax.experimental.pallas import tpu as pltpu

D_MODEL = 1024
SEQ = 16384
DEPTH = 4
CHUNK = 64
PLE_DIM = 256
FFN_HIDDEN = 2816
ML_HEADS = 4
ML_DQK = 256
ML_DV = 512
ML_INNER = ML_HEADS * ML_DV
ML_QK = ML_HEADS * ML_DQK
ML_CONV = 4
GM_BLOCK = 128
GM_GROUPS = 8
GM_WIDTH = 2 * D_MODEL
GM_GDIM = GM_WIDTH // GM_GROUPS
DN_ALPHA = (2.0 * DEPTH) ** 0.25
LN_EPS = 1e-5

V7X_MXU_DIM = 256
V7X_SUBLANES = 8
V7X_VMEM_BYTES = 64 * 1024 * 1024
VMEM_LIMIT_BYTES = V7X_VMEM_BYTES * 7 // 8

FFN_ROWS = 1024
FFN_COLS = V7X_MXU_DIM
FFN_SUB = 256
ML_ROWS = 256
GM_ROWS = 512
GM_SUB = 256

F32 = jnp.float32
BF16 = jnp.bfloat16


def _dot(a, b):
    return jnp.dot(a, b, preferred_element_type=F32)


def _dot_nt(a, b):
    return lax.dot_general(a, b, (((1,), (1,)), ((), ())), preferred_element_type=F32)


def _dot_tn(a, b):
    return lax.dot_general(a, b, (((0,), (0,)), ((), ())), preferred_element_type=F32)


def _layer_norm(z, g, b):
    mu = jnp.mean(z, axis=-1, keepdims=True)
    zc = z - mu
    var = jnp.mean(zc * zc, axis=-1, keepdims=True)
    return zc * lax.rsqrt(var + LN_EPS) * g + b


def _split3(a):
    hi = a.astype(BF16)
    r1 = a - hi.astype(F32)
    mid = r1.astype(BF16)
    lo = (r1 - mid.astype(F32)).astype(BF16)
    return hi, mid, lo


def _gelu(x):
    return 0.5 * x * (1.0 + lax.erf(x * (2.0 ** -0.5)))


def _log_sigmoid(x):
    return jnp.minimum(x, 0.0) - jnp.log1p(jnp.exp(-jnp.abs(x)))


def _ffn_kernel(with_ple, ln_row, *refs):
    if with_ple:
        (x_ref, wgu_ref, wd_ref, lng_ref, lnb_ref, p_ref, wpg_ref, bpg_ref, wpp_ref, o_ref, a_ref) = refs
    else:
        (x_ref, wgu_ref, wd_ref, lng_ref, lnb_ref, o_ref, a_ref) = refs
    xb = x_ref[...].astype(BF16)
    for c in range(FFN_HIDDEN // FFN_COLS):
        lo = c * FFN_COLS
        g = _dot(xb, wgu_ref[:, lo:lo + FFN_COLS])
        u = _dot(xb, wgu_ref[:, FFN_HIDDEN + lo:FFN_HIDDEN + lo + FFN_COLS])
        a_ref[:, lo:lo + FFN_COLS] = (jax.nn.silu(g) * u).astype(BF16)
    for r in range(FFN_ROWS // FFN_SUB):
        rs = slice(r * FFN_SUB, (r + 1) * FFN_SUB)
        y = _dot(a_ref[rs, :], wd_ref[...])
        x1 = _layer_norm(DN_ALPHA * x_ref[rs, :] + 0.5 * y,
                         lng_ref[ln_row:ln_row + 1, :], lnb_ref[ln_row:ln_row + 1, :])
        if with_ple:
            gate = jax.nn.sigmoid(_dot(x1.astype(BF16), wpg_ref[...]) + bpg_ref[...])
            emb = _dot(p_ref[rs, :].astype(BF16), wpp_ref[...])
            x1 = _layer_norm(DN_ALPHA * x1 + gate * emb, lng_ref[3:4, :], lnb_ref[3:4, :])
        o_ref[rs, :] = x1


def _resident(shape, index):
    return pl.BlockSpec(shape, lambda t: index, pipeline_mode=pl.Buffered(1))


def _ffn_call(x, wgu, wd, ln_g, ln_b, layer, ln_row, ple=None):
    S, D = x.shape
    F = FFN_HIDDEN
    rows = pl.BlockSpec((FFN_ROWS, D), lambda t: (t, 0))
    in_specs = [
        rows,
        _resident((None, D, 2 * F), (layer, 0, 0)),
        _resident((None, F, D), (layer, 0, 0)),
        _resident((None, 4, D), (layer, 0, 0)),
        _resident((None, 4, D), (layer, 0, 0)),
    ]
    args = [x, wgu, wd, ln_g, ln_b]
    if ple is not None:
        p, wpg, bpg, wpp = ple
        in_specs += [
            pl.BlockSpec((None, FFN_ROWS, PLE_DIM), lambda t: (layer, t, 0)),
            _resident((None, D, D), (layer, 0, 0)),
            _resident((None, 1, D), (layer, 0, 0)),
            _resident((None, PLE_DIM, D), (layer, 0, 0)),
        ]
        args += [p, wpg, bpg, wpp]
    return pl.pallas_call(
        functools.partial(_ffn_kernel, ple is not None, ln_row),
        grid=(S // FFN_ROWS,),
        in_specs=in_specs,
        out_specs=rows,
        out_shape=jax.ShapeDtypeStruct((S, D), F32),
        scratch_shapes=[pltpu.VMEM((FFN_ROWS, F), BF16)],
        compiler_params=pltpu.CompilerParams(
            dimension_semantics=("parallel",), vmem_limit_bytes=VMEM_LIMIT_BYTES),
        name="ffn_ple" if ple is not None else "ffn",
    )(*args)


def _mlstm_kernel(x_ref, win_ref, bin_ref, wgc_ref, wgr_ref, bgc_ref, bgr_ref, conv_ref, ng_ref,
                  wout_ref, lng_ref, lnb_ref, o_ref,
                  qk_ref, ct_ref, n_ref, m_ref, hb_ref):
    L = ML_ROWS
    HALO = V7X_SUBLANES

    @pl.when(pl.program_id(0) == 0)
    def _init():
        qk_ref[0:HALO, :] = jnp.zeros((HALO, 2 * ML_QK), F32)
        ct_ref[...] = jnp.zeros_like(ct_ref)
        n_ref[...] = jnp.zeros_like(n_ref)
        m_ref[...] = jnp.zeros_like(m_ref)

    x = x_ref[...]
    xb = x.astype(BF16)

    qk_ref[HALO:HALO + L, :] = _dot(xb, win_ref[:, 0:2 * ML_QK]) + bin_ref[:, 0:2 * ML_QK]

    g_col = _dot(xb, wgc_ref[...]) + bgc_ref[...]
    g_row = _dot_nt(wgr_ref[...], xb) + bgr_ref[...]
    ri = lax.broadcasted_iota(jnp.int32, (L, L), 0)
    ci = lax.broadcasted_iota(jnp.int32, (L, L), 1)
    causal = ri >= ci
    lower = causal.astype(BF16)
    upper = (ri <= ci).astype(BF16)
    fh, fm, fl = _split3(_log_sigmoid(g_col))
    bcum_col = _dot(lower, fh) + _dot(lower, fm) + _dot(lower, fl)
    fh, fm, fl = _split3(_log_sigmoid(g_row))
    bcum_row = _dot(fh, upper) + _dot(fm, upper) + _dot(fl, upper)

    for h in range(ML_HEADS):
        qlo = h * ML_DQK
        klo = ML_QK + h * ML_DQK
        q = jnp.zeros((L, ML_DQK), F32)
        k = jnp.zeros((L, ML_DQK), F32)
        for j in range(ML_CONV):
            rows = pl.ds(HALO - (ML_CONV - 1) + j, L)
            q = q + conv_ref[j:j + 1, qlo:qlo + ML_DQK] * qk_ref[rows, qlo:qlo + ML_DQK]
            k = k + conv_ref[j:j + 1, klo:klo + ML_DQK] * qk_ref[rows, klo:klo + ML_DQK]
        q = jax.nn.silu(q)
        k = jax.nn.silu(k) * (ML_DQK ** -0.5)
        qb = q.astype(BF16)
        kb = k.astype(BF16)
        vlo = 2 * ML_QK + h * ML_DV
        v = _dot(xb, win_ref[:, vlo:vlo + ML_DV]) + bin_ref[:, vlo:vlo + ML_DV]
        vb = v.astype(BF16)

        i_col = g_col[:, h:h + 1]
        b_col = bcum_col[:, ML_HEADS + h:ML_HEADS + h + 1]
        i_row = g_row[h:h + 1, :]
        b_row = bcum_row[ML_HEADS + h:ML_HEADS + h + 1, :]
        m_prev = m_ref[h][:, 0:1]
        ct = ct_ref[h]
        n_prev = n_ref[h]

        d_log = jnp.where(causal, b_col - b_row + i_row, -jnp.inf)
        inter = b_col + m_prev
        m_t = jnp.maximum(inter, jnp.max(d_log, axis=-1, keepdims=True))
        s_mat = _dot_nt(qb, kb) * jnp.exp(d_log - m_t)
        w_inter = jnp.exp(inter - m_t)
        num = _dot(s_mat.astype(BF16), vb) + w_inter * _dot(qb, ct.astype(BF16))
        den = jnp.sum(s_mat, axis=-1, keepdims=True) + w_inter * jnp.sum(q * n_prev, axis=-1, keepdims=True)
        hh = num / jnp.maximum(jnp.abs(den), jnp.exp(-m_t))

        b_last = b_col[L - 1:L, :]
        w_log = b_last - b_col + i_col
        m_new = jnp.maximum(b_last + m_prev, jnp.max(w_log, axis=0, keepdims=True))
        decay = jnp.exp(b_last + m_prev - m_new)
        kw = k * jnp.exp(w_log - m_new)
        ct_ref[h] = decay * ct + _dot_tn(kw.astype(BF16), vb)
        n_ref[h] = decay * n_prev + jnp.sum(kw, axis=0, keepdims=True)
        m_ref[h] = jnp.broadcast_to(m_new, m_ref.shape[1:])

        mu = jnp.mean(hh, axis=-1, keepdims=True)
        hc = hh - mu
        var = jnp.mean(hc * hc, axis=-1, keepdims=True)
        hn = hc * lax.rsqrt(var + LN_EPS) * ng_ref[:, h * ML_DV:(h + 1) * ML_DV]
        zlo = 2 * ML_QK + ML_INNER + h * ML_DV
        z = _dot(xb, win_ref[:, zlo:zlo + ML_DV]) + bin_ref[:, zlo:zlo + ML_DV]
        hb_ref[:, h * ML_DV:(h + 1) * ML_DV] = (hn * jax.nn.sigmoid(z)).astype(BF16)

    qk_ref[0:HALO, :] = qk_ref[L:L + HALO, :]

    y = _dot(hb_ref[...], wout_ref[...])
    o_ref[...] = _layer_norm(DN_ALPHA * x + y, lng_ref[1:2, :], lnb_ref[1:2, :])


def _mlstm_call(x, w_in, b_in, wg_col, wg_row, bg_col, bg_row, conv_w, norm_g, w_out, ln_g, ln_b, layer, j):
    S, D = x.shape
    L = ML_ROWS
    NM = 2 * ML_QK + 2 * ML_INNER
    rows = pl.BlockSpec((L, D), lambda t: (t, 0))
    in_specs = [
        rows,
        _resident((None, D, NM), (j, 0, 0)),
        _resident((None, 1, NM), (j, 0, 0)),
        _resident((None, D, 2 * ML_HEADS), (j, 0, 0)),
        _resident((None, 4 * ML_HEADS, D), (j, 0, 0)),
        _resident((None, 1, 2 * ML_HEADS), (j, 0, 0)),
        _resident((None, 4 * ML_HEADS, 1), (j, 0, 0)),
        _resident((None, ML_CONV, 2 * ML_QK), (j, 0, 0)),
        _resident((None, 1, ML_INNER), (j, 0, 0)),
        _resident((None, ML_INNER, D), (j, 0, 0)),
        _resident((None, 4, D), (layer, 0, 0)),
        _resident((None, 4, D), (layer, 0, 0)),
    ]
    return pl.pallas_call(
        _mlstm_kernel,
        grid=(S // L,),
        in_specs=in_specs,
        out_specs=rows,
        out_shape=jax.ShapeDtypeStruct((S, D), F32),
        scratch_shapes=[
            pltpu.VMEM((V7X_SUBLANES + L, 2 * ML_QK), F32),
            pltpu.VMEM((ML_HEADS, ML_DQK, ML_DV), F32),
            pltpu.VMEM((ML_HEADS, 1, ML_DQK), F32),
            pltpu.VMEM((ML_HEADS, 1, 128), F32),
            pltpu.VMEM((L, ML_INNER), BF16),
        ],
        compiler_params=pltpu.CompilerParams(
            dimension_semantics=("arbitrary",), vmem_limit_bytes=VMEM_LIMIT_BYTES),
        name="mlstm",
    )(x, w_in, b_in, wg_col, wg_row, bg_col, bg_row, conv_w, norm_g, w_out, ln_g, ln_b)


def _gmlp_kernel(x_ref, win_ref, bin_ref, vng_ref, vnb_ref, ws_ref, bst_ref, wout_ref, lng_ref, lnb_ref,
                 o_ref, v_ref, u_ref, y_ref):
    xb = x_ref[...].astype(BF16)
    for g in range(GM_GROUPS):
        lo = g * GM_GDIM
        vlo = GM_WIDTH + lo
        v_ref[:, lo:lo + GM_GDIM] = _gelu(_dot(xb, win_ref[:, vlo:vlo + GM_GDIM]) + bin_ref[:, vlo:vlo + GM_GDIM])
    for g in range(GM_GROUPS):
        lo = g * GM_GDIM
        u_ref[:, lo:lo + GM_GDIM] = _gelu(_dot(xb, win_ref[:, lo:lo + GM_GDIM]) + bin_ref[:, lo:lo + GM_GDIM])
    v = v_ref[...]
    mu = jnp.mean(v, axis=-1, keepdims=True)
    vc = v - mu
    rstd = lax.rsqrt(jnp.mean(vc * vc, axis=-1, keepdims=True) + LN_EPS)

    blk_r = lax.broadcasted_iota(jnp.int32, (GM_BLOCK, GM_BLOCK), 0) // CHUNK
    blk_c = lax.broadcasted_iota(jnp.int32, (GM_BLOCK, GM_BLOCK), 1) // CHUNK
    chunk_causal = blk_r >= blk_c
    for g in range(GM_GROUPS):
        lo = g * GM_GDIM
        vn = ((v_ref[:, lo:lo + GM_GDIM] - mu) * rstd * vng_ref[:, lo:lo + GM_GDIM]
              + vnb_ref[:, lo:lo + GM_GDIM]).astype(BF16)
        wsm = jnp.where(chunk_causal, ws_ref[g], 0.0).astype(BF16)
        bias = bst_ref[:, g:g + 1]
        for nb in range(GM_ROWS // GM_BLOCK):
            r0 = nb * GM_BLOCK
            s = _dot(wsm, vn[r0:r0 + GM_BLOCK, :]) + bias
            y_ref[r0:r0 + GM_BLOCK, lo:lo + GM_GDIM] = (u_ref[r0:r0 + GM_BLOCK, lo:lo + GM_GDIM] * s).astype(BF16)
    for r in range(GM_ROWS // GM_SUB):
        rs = slice(r * GM_SUB, (r + 1) * GM_SUB)
        y = _dot(y_ref[rs, :], wout_ref[...])
        o_ref[rs, :] = _layer_norm(DN_ALPHA * x_ref[rs, :] + y, lng_ref[1:2, :], lnb_ref[1:2, :])


def _gmlp_call(x, w_in, b_in, vn_g, vn_b, ws, bs_t, w_out, ln_g, ln_b, layer, j):
    S, D = x.shape
    rows = pl.BlockSpec((GM_ROWS, D), lambda t: (t, 0))
    in_specs = [
        rows,
        _resident((None, D, 2 * GM_WIDTH), (j, 0, 0)),
        _resident((None, 1, 2 * GM_WIDTH), (j, 0, 0)),
        _resident((None, 1, GM_WIDTH), (j, 0, 0)),
        _resident((None, 1, GM_WIDTH), (j, 0, 0)),
        _resident((None, GM_GROUPS, GM_BLOCK, GM_BLOCK), (j, 0, 0, 0)),
        _resident((None, GM_BLOCK, GM_GROUPS), (j, 0, 0)),
        _resident((None, GM_WIDTH, D), (j, 0, 0)),
        _resident((None, 4, D), (layer, 0, 0)),
        _resident((None, 4, D), (layer, 0, 0)),
    ]
    return pl.pallas_call(
        _gmlp_kernel,
        grid=(S // GM_ROWS,),
        in_specs=in_specs,
        out_specs=rows,
        out_shape=jax.ShapeDtypeStruct((S, D), F32),
        scratch_shapes=[
            pltpu.VMEM((GM_ROWS, GM_WIDTH), F32),
            pltpu.VMEM((GM_ROWS, GM_WIDTH), F32),
            pltpu.VMEM((GM_ROWS, GM_WIDTH), BF16),
        ],
        compiler_params=pltpu.CompilerParams(
            dimension_semantics=("parallel",), vmem_limit_bytes=VMEM_LIMIT_BYTES),
        name="gmlp",
    )(x, w_in, b_in, vn_g, vn_b, ws, bs_t, w_out, ln_g, ln_b)


def kernel(x, p, ln_g, ln_b, ffn1_wgu, ffn1_wd, ffn2_wgu, ffn2_wd, ml_w_in, ml_b_in, ml_conv, ml_norm_g, ml_w_out, gm_w_in, gm_b_in, gm_vn_g, gm_vn_b, gm_ws, gm_bs, gm_w_out, ple_wp, ple_wg, ple_bg):
    B, S, D = x.shape
    assert (B, S, D) == (1, SEQ, D_MODEL)
    xs = x.reshape(S, D)
    ps = p.reshape(DEPTH, S, PLE_DIM)

    ffn1_wgu_b, ffn1_wd_b = ffn1_wgu.astype(BF16), ffn1_wd.astype(BF16)
    ffn2_wgu_b, ffn2_wd_b = ffn2_wgu.astype(BF16), ffn2_wd.astype(BF16)
    ple_wg_b, ple_wp_b = ple_wg.astype(BF16), ple_wp.astype(BF16)
    ple_bg_r = ple_bg[:, None, :]

    NM = 2 * ML_QK + 2 * ML_INNER
    ml_w_main = ml_w_in.astype(BF16)
    ml_b_main = ml_b_in[:, None, :NM]
    ml_wg_col = ml_w_in[:, :, NM:].astype(BF16)
    ml_wg_row = jnp.pad(jnp.swapaxes(ml_w_in[:, :, NM:], 1, 2),
                        ((0, 0), (0, 2 * ML_HEADS), (0, 0))).astype(BF16)
    ml_bg_col = ml_b_in[:, None, NM:]
    ml_bg_row = jnp.pad(ml_b_in[:, NM:], ((0, 0), (0, 2 * ML_HEADS)))[:, :, None]
    ml_norm_g_r = ml_norm_g[:, None, :]
    ml_w_out_b = ml_w_out.astype(BF16)

    gm_w_in_b, gm_w_out_b = gm_w_in.astype(BF16), gm_w_out.astype(BF16)
    gm_b_in_r = gm_b_in[:, None, :]
    gm_vn_g_r, gm_vn_b_r = gm_vn_g[:, None, :], gm_vn_b[:, None, :]
    gm_bs_t = jnp.swapaxes(gm_bs, 1, 2)

    for i in range(DEPTH):
        xs = _ffn_call(xs, ffn1_wgu_b, ffn1_wd_b, ln_g, ln_b, i, 0)
        j = i // 2
        if i % 2 == 0:
            xs = _mlstm_call(xs, ml_w_main, ml_b_main, ml_wg_col, ml_wg_row, ml_bg_col, ml_bg_row,
                             ml_conv, ml_norm_g_r, ml_w_out_b, ln_g, ln_b, i, j)
        else:
            xs = _gmlp_call(xs, gm_w_in_b, gm_b_in_r, gm_vn_g_r, gm_vn_b_r, gm_ws, gm_bs_t,
                            gm_w_out_b, ln_g, ln_b, i, j)
        xs = _ffn_call(xs, ffn2_wgu_b, ffn2_wd_b, ln_g, ln_b, i, 2,
                       ple=(ps, ple_wg_b, ple_bg_r, ple_wp_b))
    return xs.reshape(B, S, D)
```

```python
import functools

import jax
import jax.numpy as jnp
from jax import lax
from jax.experimental import pallas as pl
from jax.experimental.pallas import tpu as pltpu

D_MODEL = 1024
SEQ = 16384
DEPTH = 4
CHUNK = 64
PLE_DIM = 256
FFN_HIDDEN = 2816
ML_HEADS = 4
ML_DQK = 256
ML_DV = 512
ML_INNER = ML_HEADS * ML_DV
ML_QK = ML_HEADS * ML_DQK
ML_MAIN = 2 * ML_QK + 2 * ML_INNER
ML_CONV = 4
GM_BLOCK = 128
GM_GROUPS = 8
GM_WIDTH = 2 * D_MODEL
GM_GDIM = GM_WIDTH // GM_GROUPS
DN_ALPHA = (2.0 * DEPTH) ** 0.25
LN_EPS = 1e-5

V7X_MXU_DIM = 256
V7X_SUBLANES = 8
V7X_BF16_ROWS = 16
V7X_VMEM_BYTES = 64 * 1024 * 1024
VMEM_LIMIT_BYTES = V7X_VMEM_BYTES * 7 // 8

FFN_ROWS = 1024
FFN_PLE_ROWS = 512
FFN_COLS = V7X_MXU_DIM
FFN_SUB = 256
ML_ROWS = 256
GM_ROWS = 512
GM_SUB = 256

F32 = jnp.float32
BF16 = jnp.bfloat16


def _dot(a, b):
    return jnp.dot(a, b, preferred_element_type=F32)


def _dot_nt(a, b):
    return lax.dot_general(a, b, (((1,), (1,)), ((), ())), preferred_element_type=F32)


def _dot_tn(a, b):
    return lax.dot_general(a, b, (((0,), (0,)), ((), ())), preferred_element_type=F32)


def _layer_norm(z, g, b):
    mu = jnp.mean(z, axis=-1, keepdims=True)
    zc = z - mu
    var = jnp.mean(zc * zc, axis=-1, keepdims=True)
    return zc * lax.rsqrt(var + LN_EPS) * g + b


def _split3(a):
    hi = a.astype(BF16)
    r1 = a - hi.astype(F32)
    mid = r1.astype(BF16)
    lo = (r1 - mid.astype(F32)).astype(BF16)
    return hi, mid, lo


def _gelu(x):
    return 0.5 * x * (1.0 + lax.erf(x * (2.0 ** -0.5)))


def _log_sigmoid(x):
    return jnp.minimum(x, 0.0) - jnp.log1p(jnp.exp(-jnp.abs(x)))


def _resident(shape, index):
    return pl.BlockSpec(shape, lambda t: index, pipeline_mode=pl.Buffered(1))


def _cast_plan(weights, steps):
    in_specs, out_specs, out_shapes, args = [], [], [], []
    for w, layer in weights:
        _, K, N = w.shape
        hold = 1
        while K % (steps // hold) or (K // (steps // hold)) % V7X_BF16_ROWS:
            hold *= 2
        rows = K // (steps // hold)
        in_specs.append(pl.BlockSpec((None, rows, N), lambda t, layer=layer, hold=hold: (layer, t // hold, 0)))
        out_specs.append(pl.BlockSpec((rows, N), lambda t, hold=hold: (t // hold, 0)))
        out_shapes.append(jax.ShapeDtypeStruct((K, N), BF16))
        args.append(w)
    return in_specs, out_specs, out_shapes, args


def _split_refs(refs, n_in, n_cast):
    a, b, c = n_in, n_in + n_cast, n_in + 2 * n_cast + 1
    return refs[:a], refs[a:b], refs[b], refs[b + 1:c], refs[c:]


def _run_casts(cast_in, cast_out):
    for src, dst in zip(cast_in, cast_out):
        dst[...] = src[...].astype(BF16)


def _call(kernel_fn, name, x, row_spec, main_specs, main_args, cast_weights, scratch_shapes):
    S, D = x.shape
    steps = S // row_spec.block_shape[0]
    c_in, c_out, c_shapes, c_args = _cast_plan(cast_weights, steps)
    outs = pl.pallas_call(
        functools.partial(kernel_fn, 1 + len(main_args), len(c_args)),
        grid=(steps,),
        in_specs=[row_spec] + main_specs + c_in,
        out_specs=[row_spec] + c_out,
        out_shape=[jax.ShapeDtypeStruct((S, D), F32)] + c_shapes,
        scratch_shapes=scratch_shapes,
        compiler_params=pltpu.CompilerParams(
            dimension_semantics=("arbitrary",), vmem_limit_bytes=VMEM_LIMIT_BYTES),
        name=name,
    )(x, *main_args, *c_args)
    return outs[0], outs[1:]


def _ffn_kernel(with_ple, ln_row, n_in, n_cast, *refs):
    ins, cast_in, o_ref, cast_out, (a_ref,) = _split_refs(refs, n_in, n_cast)
    if with_ple:
        x_ref, wgu_ref, wd_ref, lng_ref, lnb_ref, p_ref, wpg_ref, bpg_ref, wpp_ref = ins
    else:
        x_ref, wgu_ref, wd_ref, lng_ref, lnb_ref = ins
    _run_casts(cast_in, cast_out)
    xb = x_ref[...].astype(BF16)
    for c in range(FFN_HIDDEN // FFN_COLS):
        lo = c * FFN_COLS
        g = _dot(xb, wgu_ref[:, lo:lo + FFN_COLS])
        u = _dot(xb, wgu_ref[:, FFN_HIDDEN + lo:FFN_HIDDEN + lo + FFN_COLS])
        a_ref[:, lo:lo + FFN_COLS] = (jax.nn.silu(g) * u).astype(BF16)
    for r in range(a_ref.shape[0] // FFN_SUB):
        rs = slice(r * FFN_SUB, (r + 1) * FFN_SUB)
        y = _dot(a_ref[rs, :], wd_ref[...])
        x1 = _layer_norm(DN_ALPHA * x_ref[rs, :] + 0.5 * y,
                         lng_ref[ln_row:ln_row + 1, :], lnb_ref[ln_row:ln_row + 1, :])
        if with_ple:
            gate = jax.nn.sigmoid(_dot(x1.astype(BF16), wpg_ref[...]) + bpg_ref[...])
            emb = _dot(p_ref[rs, :].astype(BF16), wpp_ref[...])
            x1 = _layer_norm(DN_ALPHA * x1 + gate * emb, lng_ref[3:4, :], lnb_ref[3:4, :])
        o_ref[rs, :] = x1


def _ffn_call(x, wgu, wd, ln_g, ln_b, layer, ln_row, cast_weights, ple=None):
    S, D = x.shape
    F = FFN_HIDDEN
    tm = FFN_ROWS if ple is None else FFN_PLE_ROWS
    main_specs = [
        _resident((D, 2 * F), (0, 0)),
        _resident((F, D), (0, 0)),
        _resident((None, 4, D), (layer, 0, 0)),
        _resident((None, 4, D), (layer, 0, 0)),
    ]
    main_args = [wgu, wd, ln_g, ln_b]
    if ple is not None:
        p, wpg, bpg, wpp = ple
        main_specs += [
            pl.BlockSpec((None, tm, PLE_DIM), lambda t: (layer, t, 0)),
            _resident((D, D), (0, 0)),
            _resident((None, 1, D), (layer, 0, 0)),
            _resident((PLE_DIM, D), (0, 0)),
        ]
        main_args += [p, wpg, bpg, wpp]
    return _call(functools.partial(_ffn_kernel, ple is not None, ln_row),
                 "ffn_ple" if ple is not None else "ffn",
                 x, pl.BlockSpec((tm, D), lambda t: (t, 0)), main_specs, main_args, cast_weights,
                 [pltpu.VMEM((tm, F), BF16)])


def _mlstm_kernel(n_in, n_cast, *refs):
    ins, cast_in, o_ref, cast_out, scratch = _split_refs(refs, n_in, n_cast)
    (x_ref, win_ref, bin_ref, wgc_ref, wgr_ref, bgc_ref, bgr_ref, conv_ref, ng_ref,
     wout_ref, lng_ref, lnb_ref) = ins
    qk_ref, ct_ref, n_ref, m_ref, hb_ref = scratch
    L = ML_ROWS
    HALO = V7X_SUBLANES

    @pl.when(pl.program_id(0) == 0)
    def _init():
        qk_ref[0:HALO, :] = jnp.zeros((HALO, 2 * ML_QK), F32)
        ct_ref[...] = jnp.zeros_like(ct_ref)
        n_ref[...] = jnp.zeros_like(n_ref)
        m_ref[...] = jnp.zeros_like(m_ref)

    _run_casts(cast_in, cast_out)
    x = x_ref[...]
    xb = x.astype(BF16)

    qk_ref[HALO:HALO + L, :] = _dot(xb, win_ref[:, 0:2 * ML_QK]) + bin_ref[:, 0:2 * ML_QK]

    g_col = _dot(xb, wgc_ref[...]) + bgc_ref[...]
    g_row = _dot_nt(wgr_ref[...], xb) + bgr_ref[...]
    ri = lax.broadcasted_iota(jnp.int32, (L, L), 0)
    ci = lax.broadcasted_iota(jnp.int32, (L, L), 1)
    causal = ri >= ci
    lower = causal.astype(BF16)
    upper = (ri <= ci).astype(BF16)
    fh, fm, fl = _split3(_log_sigmoid(g_col))
    bcum_col = _dot(lower, fh) + _dot(lower, fm) + _dot(lower, fl)
    fh, fm, fl = _split3(_log_sigmoid(g_row))
    bcum_row = _dot(fh, upper) + _dot(fm, upper) + _dot(fl, upper)

    for h in range(ML_HEADS):
        qlo = h * ML_DQK
        klo = ML_QK + h * ML_DQK
        q = jnp.zeros((L, ML_DQK), F32)
        k = jnp.zeros((L, ML_DQK), F32)
        for j in range(ML_CONV):
            rows = pl.ds(HALO - (ML_CONV - 1) + j, L)
            q = q + conv_ref[j:j + 1, qlo:qlo + ML_DQK] * qk_ref[rows, qlo:qlo + ML_DQK]
            k = k + conv_ref[j:j + 1, klo:klo + ML_DQK] * qk_ref[rows, klo:klo + ML_DQK]
        q = jax.nn.silu(q)
        k = jax.nn.silu(k) * (ML_DQK ** -0.5)
        qb = q.astype(BF16)
        kb = k.astype(BF16)
        vlo = 2 * ML_QK + h * ML_DV
        v = _dot(xb, win_ref[:, vlo:vlo + ML_DV]) + bin_ref[:, vlo:vlo + ML_DV]
        vb = v.astype(BF16)

        i_col = g_col[:, h:h + 1]
        b_col = bcum_col[:, ML_HEADS + h:ML_HEADS + h + 1]
        i_row = g_row[h:h + 1, :]
        b_row = bcum_row[ML_HEADS + h:ML_HEADS + h + 1, :]
        m_prev = m_ref[h][:, 0:1]
        ct = ct_ref[h]
        n_prev = n_ref[h]

        d_log = jnp.where(causal, b_col - b_row + i_row, -jnp.inf)
        inter = b_col + m_prev
        m_t = jnp.maximum(inter, jnp.max(d_log, axis=-1, keepdims=True))
        s_mat = _dot_nt(qb, kb) * jnp.exp(d_log - m_t)
        w_inter = jnp.exp(inter - m_t)
        num = _dot(s_mat.astype(BF16), vb) + w_inter * _dot(qb, ct.astype(BF16))
        den = jnp.sum(s_mat, axis=-1, keepdims=True) + w_inter * jnp.sum(q * n_prev, axis=-1, keepdims=True)
        hh = num / jnp.maximum(jnp.abs(den), jnp.exp(-m_t))

        b_last = b_col[L - 1:L, :]
        w_log = b_last - b_col + i_col
        m_new = jnp.maximum(b_last + m_prev, jnp.max(w_log, axis=0, keepdims=True))
        decay = jnp.exp(b_last + m_prev - m_new)
        kw = k * jnp.exp(w_log - m_new)
        ct_ref[h] = decay * ct + _dot_tn(kw.astype(BF16), vb)
        n_ref[h] = decay * n_prev + jnp.sum(kw, axis=0, keepdims=True)
        m_ref[h] = jnp.broadcast_to(m_new, m_ref.shape[1:])

        mu = jnp.mean(hh, axis=-1, keepdims=True)
        hc = hh - mu
        var = jnp.mean(hc * hc, axis=-1, keepdims=True)
        hn = hc * lax.rsqrt(var + LN_EPS) * ng_ref[:, h * ML_DV:(h + 1) * ML_DV]
        zlo = 2 * ML_QK + ML_INNER + h * ML_DV
        z = _dot(xb, win_ref[:, zlo:zlo + ML_DV]) + bin_ref[:, zlo:zlo + ML_DV]
        hb_ref[:, h * ML_DV:(h + 1) * ML_DV] = (hn * jax.nn.sigmoid(z)).astype(BF16)

    qk_ref[0:HALO, :] = qk_ref[L:L + HALO, :]

    y = _dot(hb_ref[...], wout_ref[...])
    o_ref[...] = _layer_norm(DN_ALPHA * x + y, lng_ref[1:2, :], lnb_ref[1:2, :])


def _mlstm_call(x, w_in, b_in, wg_col, wg_row, bg_col, bg_row, conv_w, norm_g, w_out, ln_g, ln_b, layer, j,
                cast_weights):
    S, D = x.shape
    L = ML_ROWS
    main_specs = [
        _resident((D, ML_MAIN), (0, 0)),
        _resident((None, 1, ML_MAIN), (j, 0, 0)),
        _resident((None, D, 2 * ML_HEADS), (j, 0, 0)),
        _resident((None, 4 * ML_HEADS, D), (j, 0, 0)),
        _resident((None, 1, 2 * ML_HEADS), (j, 0, 0)),
        _resident((None, 4 * ML_HEADS, 1), (j, 0, 0)),
        _resident((None, ML_CONV, 2 * ML_QK), (j, 0, 0)),
        _resident((None, 1, ML_INNER), (j, 0, 0)),
        _resident((ML_INNER, D), (0, 0)),
        _resident((None, 4, D), (layer, 0, 0)),
        _resident((None, 4, D), (layer, 0, 0)),
    ]
    main_args = [w_in, b_in, wg_col, wg_row, bg_col, bg_row, conv_w, norm_g, w_out, ln_g, ln_b]
    scratch = [
        pltpu.VMEM((V7X_SUBLANES + L, 2 * ML_QK), F32),
        pltpu.VMEM((ML_HEADS, ML_DQK, ML_DV), F32),
        pltpu.VMEM((ML_HEADS, 1, ML_DQK), F32),
        pltpu.VMEM((ML_HEADS, 1, 128), F32),
        pltpu.VMEM((L, ML_INNER), BF16),
    ]
    return _call(_mlstm_kernel, "mlstm", x, pl.BlockSpec((L, D), lambda t: (t, 0)),
                 main_specs, main_args, cast_weights, scratch)


def _gmlp_kernel(n_in, n_cast, *refs):
    ins, cast_in, o_ref, cast_out, (v_ref, u_ref, y_ref) = _split_refs(refs, n_in, n_cast)
    x_ref, win_ref, bin_ref, vng_ref, vnb_ref, ws_ref, bst_ref, wout_ref, lng_ref, lnb_ref = ins
    _run_casts(cast_in, cast_out)
    xb = x_ref[...].astype(BF16)
    for g in range(GM_GROUPS):
        lo = g * GM_GDIM
        vlo = GM_WIDTH + lo
        v_ref[:, lo:lo + GM_GDIM] = _gelu(_dot(xb, win_ref[:, vlo:vlo + GM_GDIM]) + bin_ref[:, vlo:vlo + GM_GDIM])
    for g in range(GM_GROUPS):
        lo = g * GM_GDIM
        u_ref[:, lo:lo + GM_GDIM] = _gelu(_dot(xb, win_ref[:, lo:lo + GM_GDIM]) + bin_ref[:, lo:lo + GM_GDIM])
    v = v_ref[...]
    mu = jnp.mean(v, axis=-1, keepdims=True)
    vc = v - mu
    rstd = lax.rsqrt(jnp.mean(vc * vc, axis=-1, keepdims=True) + LN_EPS)

    blk_r = lax.broadcasted_iota(jnp.int32, (GM_BLOCK, GM_BLOCK), 0) // CHUNK
    blk_c = lax.broadcasted_iota(jnp.int32, (GM_BLOCK, GM_BLOCK), 1) // CHUNK
    chunk_causal = blk_r >= blk_c
    for g in range(GM_GROUPS):
        lo = g * GM_GDIM
        vn = ((v_ref[:, lo:lo + GM_GDIM] - mu) * rstd * vng_ref[:, lo:lo + GM_GDIM]
              + vnb_ref[:, lo:lo + GM_GDIM]).astype(BF16)
        wsm = jnp.where(chunk_causal, ws_ref[g], 0.0).astype(BF16)
        bias = bst_ref[:, g:g + 1]
        for nb in range(GM_ROWS // GM_BLOCK):
            r0 = nb * GM_BLOCK
            s = _dot(wsm, vn[r0:r0 + GM_BLOCK, :]) + bias
            y_ref[r0:r0 + GM_BLOCK, lo:lo + GM_GDIM] = (u_ref[r0:r0 + GM_BLOCK, lo:lo + GM_GDIM] * s).astype(BF16)
    for r in range(GM_ROWS // GM_SUB):
        rs = slice(r * GM_SUB, (r + 1) * GM_SUB)
        y = _dot(y_ref[rs, :], wout_ref[...])
        o_ref[rs, :] = _layer_norm(DN_ALPHA * x_ref[rs, :] + y, lng_ref[1:2, :], lnb_ref[1:2, :])


def _gmlp_call(x, w_in, b_in, vn_g, vn_b, ws, bs_t, w_out, ln_g, ln_b, layer, j, cast_weights):
    S, D = x.shape
    main_specs = [
        _resident((D, 2 * GM_WIDTH), (0, 0)),
        _resident((None, 1, 2 * GM_WIDTH), (j, 0, 0)),
        _resident((None, 1, GM_WIDTH), (j, 0, 0)),
        _resident((None, 1, GM_WIDTH), (j, 0, 0)),
        _resident((None, GM_GROUPS, GM_BLOCK, GM_BLOCK), (j, 0, 0, 0)),
        _resident((None, GM_BLOCK, GM_GROUPS), (j, 0, 0)),
        _resident((GM_WIDTH, D), (0, 0)),
        _resident((None, 4, D), (layer, 0, 0)),
        _resident((None, 4, D), (layer, 0, 0)),
    ]
    main_args = [w_in, b_in, vn_g, vn_b, ws, bs_t, w_out, ln_g, ln_b]
    scratch = [
        pltpu.VMEM((GM_ROWS, GM_WIDTH), F32),
        pltpu.VMEM((GM_ROWS, GM_WIDTH), F32),
        pltpu.VMEM((GM_ROWS, GM_WIDTH), BF16),
    ]
    return _call(_gmlp_kernel, "gmlp", x, pl.BlockSpec((GM_ROWS, D), lambda t: (t, 0)),
                 main_specs, main_args, cast_weights, scratch)


def kernel(x, p, ln_g, ln_b, ffn1_wgu, ffn1_wd, ffn2_wgu, ffn2_wd, ml_w_in, ml_b_in, ml_conv, ml_norm_g, ml_w_out, gm_w_in, gm_b_in, gm_vn_g, gm_vn_b, gm_ws, gm_bs, gm_w_out, ple_wp, ple_wg, ple_bg):
    B, S, D = x.shape
    assert (B, S, D) == (1, SEQ, D_MODEL)
    xs = x.reshape(S, D)
    ps = p.reshape(DEPTH, S, PLE_DIM)

    ple_bg_r = ple_bg[:, None, :]
    ml_b_main = ml_b_in[:, None, :ML_MAIN]
    ml_wg_col = ml_w_in[:, :, ML_MAIN:].astype(BF16)
    ml_wg_row = jnp.pad(jnp.swapaxes(ml_w_in[:, :, ML_MAIN:], 1, 2),
                        ((0, 0), (0, 2 * ML_HEADS), (0, 0))).astype(BF16)
    ml_bg_col = ml_b_in[:, None, ML_MAIN:]
    ml_bg_row = jnp.pad(ml_b_in[:, ML_MAIN:], ((0, 0), (0, 2 * ML_HEADS)))[:, :, None]
    ml_norm_g_r = ml_norm_g[:, None, :]
    gm_b_in_r = gm_b_in[:, None, :]
    gm_vn_g_r, gm_vn_b_r = gm_vn_g[:, None, :], gm_vn_b[:, None, :]
    gm_bs_t = jnp.swapaxes(gm_bs, 1, 2)

    ffn1_w = (ffn1_wgu[0].astype(BF16), ffn1_wd[0].astype(BF16))
    for i in range(DEPTH):
        j = i // 2
        mixer_f32 = [(ml_w_in, j), (ml_w_out, j)] if i % 2 == 0 else [(gm_w_in, j), (gm_w_out, j)]
        xs, mixer_w = _ffn_call(xs, *ffn1_w, ln_g, ln_b, i, 0, mixer_f32)
        ffn2_f32 = [(ffn2_wgu, i), (ffn2_wd, i), (ple_wg, i), (ple_wp, i)]
        if i % 2 == 0:
            xs, ffn2_w = _mlstm_call(xs, mixer_w[0], ml_b_main, ml_wg_col, ml_wg_row, ml_bg_col, ml_bg_row,
                                     ml_conv, ml_norm_g_r, mixer_w[1], ln_g, ln_b, i, j, ffn2_f32)
        else:
            xs, ffn2_w = _gmlp_call(xs, mixer_w[0], gm_b_in_r, gm_vn_g_r, gm_vn_b_r, gm_ws, gm_bs_t,
                                    mixer_w[1], ln_g, ln_b, i, j, ffn2_f32)
        next_f32 = [(ffn1_wgu, i + 1), (ffn1_wd, i + 1)] if i + 1 < DEPTH else []
        xs, ffn1_w = _ffn_call(xs, ffn2_w[0], ffn2_w[1], ln_g, ln_b, i, 2, next_f32,
                               ple=(ps, ffn2_w[2], ple_bg_r, ffn2_w[3]))
    return xs.reshape(B, S, D)
```

```python
import functools

import jax
import jax.numpy as jnp
from jax import lax
from jax.experimental import pallas as pl
from jax.experimental.pallas import tpu as pltpu

D_MODEL = 1024
SEQ = 16384
DEPTH = 4
CHUNK = 64
PLE_DIM = 256
FFN_HIDDEN = 2816
ML_HEADS = 4
ML_DQK = 256
ML_DV = 512
ML_INNER = ML_HEADS * ML_DV
ML_QK = ML_HEADS * ML_DQK
ML_MAIN = 2 * ML_QK + 2 * ML_INNER
ML_CONV = 4
GM_BLOCK = 128
GM_GROUPS = 8
GM_WIDTH = 2 * D_MODEL
GM_GDIM = GM_WIDTH // GM_GROUPS
DN_ALPHA = (2.0 * DEPTH) ** 0.25
LN_EPS = 1e-5

V7X_MXU_DIM = 256
V7X_SUBLANES = 8
V7X_BF16_ROWS = 16
V7X_VMEM_BYTES = 64 * 1024 * 1024
VMEM_LIMIT_BYTES = V7X_VMEM_BYTES * 7 // 8

FFN_ROWS = 1024
FFN_PLE_ROWS = 512
FFN_COLS = V7X_MXU_DIM
FFN_SUB = 256
ML_ROWS = 256
GM_ROWS = 512
GM_SUB = 256
WCHUNK = 512

F32 = jnp.float32
BF16 = jnp.bfloat16


def _dot(a, b):
    return jnp.dot(a, b, preferred_element_type=F32)


def _dot_nt(a, b):
    return lax.dot_general(a, b, (((1,), (1,)), ((), ())), preferred_element_type=F32)


def _dot_tn(a, b):
    return lax.dot_general(a, b, (((0,), (0,)), ((), ())), preferred_element_type=F32)


def _layer_norm(z, g, b):
    mu = jnp.mean(z, axis=-1, keepdims=True)
    zc = z - mu
    var = jnp.mean(zc * zc, axis=-1, keepdims=True)
    return zc * lax.rsqrt(var + LN_EPS) * g + b


def _split3(a):
    hi = a.astype(BF16)
    r1 = a - hi.astype(F32)
    mid = r1.astype(BF16)
    lo = (r1 - mid.astype(F32)).astype(BF16)
    return hi, mid, lo


def _gelu(x):
    return 0.5 * x * (1.0 + lax.erf(x * (2.0 ** -0.5)))


def _log_sigmoid(x):
    return jnp.minimum(x, 0.0) - jnp.log1p(jnp.exp(-jnp.abs(x)))


def _resident(shape, index):
    return pl.BlockSpec(shape, lambda t: index, pipeline_mode=pl.Buffered(1))


def _cast_plan(weights, steps):
    in_specs, out_specs, out_shapes, args = [], [], [], []
    for kind, w, layer in weights:
        args.append(w)
        if kind == "transposed":
            _, N, K = w.shape
            last = N // WCHUNK - 1
            assert steps > last
            in_specs.append(pl.BlockSpec((None, WCHUNK, K),
                                         lambda t, layer=layer, last=last: (layer, jnp.minimum(t, last), 0)))
            out_specs.append(pl.BlockSpec((None, K, WCHUNK), lambda t, last=last: (jnp.minimum(t, last), 0, 0)))
            out_shapes.append(jax.ShapeDtypeStruct((last + 1, K, WCHUNK), BF16))
            continue
        _, K, N = w.shape
        hold = 1
        while K % (steps // hold) or (K // (steps // hold)) % V7X_BF16_ROWS:
            hold *= 2
        rows = K // (steps // hold)
        in_specs.append(pl.BlockSpec((None, rows, N), lambda t, layer=layer, hold=hold: (layer, t // hold, 0)))
        if kind == "halves":
            out_specs.append(pl.BlockSpec((2, rows, N // 2), lambda t, hold=hold: (0, t // hold, 0)))
            out_shapes.append(jax.ShapeDtypeStruct((2, K, N // 2), BF16))
        else:
            out_specs.append(pl.BlockSpec((rows, N), lambda t, hold=hold: (t // hold, 0)))
            out_shapes.append(jax.ShapeDtypeStruct((K, N), BF16))
    return in_specs, out_specs, out_shapes, args


def _split_refs(refs, n_in, n_cast):
    a, b, c = n_in, n_in + n_cast, n_in + 2 * n_cast + 1
    return refs[:a], refs[a:b], refs[b], refs[b + 1:c], refs[c:]


def _run_casts(cast_in, cast_out):
    for src, dst in zip(cast_in, cast_out):
        if len(dst.shape) == 3:
            half = dst.shape[2]
            dst[0] = src[:, :half].astype(BF16)
            dst[1] = src[:, half:].astype(BF16)
        elif dst.shape != src.shape:
            dst[...] = src[...].T.astype(BF16)
        else:
            dst[...] = src[...].astype(BF16)


def _call(kernel_fn, name, x, row_spec, main_specs, main_args, cast_weights, scratch_shapes):
    S, D = x.shape
    steps = S // row_spec.block_shape[0]
    c_in, c_out, c_shapes, c_args = _cast_plan(cast_weights, steps)
    outs = pl.pallas_call(
        functools.partial(kernel_fn, 1 + len(main_args), len(c_args)),
        grid=(steps,),
        in_specs=[row_spec] + main_specs + c_in,
        out_specs=[row_spec] + c_out,
        out_shape=[jax.ShapeDtypeStruct((S, D), F32)] + c_shapes,
        scratch_shapes=scratch_shapes,
        compiler_params=pltpu.CompilerParams(
            dimension_semantics=("arbitrary",), vmem_limit_bytes=VMEM_LIMIT_BYTES),
        name=name,
    )(x, *main_args, *c_args)
    return outs[0], outs[1:]


def _ffn_kernel(with_ple, ln_row, n_in, n_cast, *refs):
    ins, cast_in, o_ref, cast_out, (a_ref,) = _split_refs(refs, n_in, n_cast)
    if with_ple:
        x_ref, wgu_ref, wd_ref, lng_ref, lnb_ref, p_ref, wpg_ref, bpg_ref, wpp_ref = ins
    else:
        x_ref, wgu_ref, wd_ref, lng_ref, lnb_ref = ins
    _run_casts(cast_in, cast_out)
    xb = x_ref[...].astype(BF16)
    for c in range(FFN_HIDDEN // FFN_COLS):
        lo = c * FFN_COLS
        g = _dot(xb, wgu_ref[:, lo:lo + FFN_COLS])
        u = _dot(xb, wgu_ref[:, FFN_HIDDEN + lo:FFN_HIDDEN + lo + FFN_COLS])
        a_ref[:, lo:lo + FFN_COLS] = (jax.nn.silu(g) * u).astype(BF16)
    for r in range(a_ref.shape[0] // FFN_SUB):
        rs = slice(r * FFN_SUB, (r + 1) * FFN_SUB)
        y = _dot(a_ref[rs, :], wd_ref[...])
        x1 = _layer_norm(DN_ALPHA * x_ref[rs, :] + 0.5 * y,
                         lng_ref[ln_row:ln_row + 1, :], lnb_ref[ln_row:ln_row + 1, :])
        if with_ple:
            gate = jax.nn.sigmoid(_dot(x1.astype(BF16), wpg_ref[...]) + bpg_ref[...])
            emb = _dot(p_ref[rs, :].astype(BF16), wpp_ref[...])
            x1 = _layer_norm(DN_ALPHA * x1 + gate * emb, lng_ref[3:4, :], lnb_ref[3:4, :])
        o_ref[rs, :] = x1


def _ffn_call(x, wgu, wd, ln_g, ln_b, layer, ln_row, cast_weights, ple=None):
    S, D = x.shape
    F = FFN_HIDDEN
    tm = FFN_ROWS if ple is None else FFN_PLE_ROWS
    main_specs = [
        _resident((D, 2 * F), (0, 0)),
        _resident((F, D), (0, 0)),
        _resident((None, 4, D), (layer, 0, 0)),
        _resident((None, 4, D), (layer, 0, 0)),
    ]
    main_args = [wgu, wd, ln_g, ln_b]
    if ple is not None:
        p, wpg, bpg, wpp = ple
        main_specs += [
            pl.BlockSpec((None, tm, PLE_DIM), lambda t: (layer, t, 0)),
            _resident((D, D), (0, 0)),
            _resident((None, 1, D), (layer, 0, 0)),
            _resident((PLE_DIM, D), (0, 0)),
        ]
        main_args += [p, wpg, bpg, wpp]
    return _call(functools.partial(_ffn_kernel, ple is not None, ln_row),
                 "ffn_ple" if ple is not None else "ffn",
                 x, pl.BlockSpec((tm, D), lambda t: (t, 0)), main_specs, main_args, cast_weights,
                 [pltpu.VMEM((tm, F), BF16)])


def _mlstm_kernel(n_in, n_cast, *refs):
    ins, cast_in, o_ref, cast_out, scratch = _split_refs(refs, n_in, n_cast)
    (x_ref, win_ref, bin_ref, wgc_ref, wgr_ref, bgc_ref, bgr_ref, conv_ref, ng_ref,
     wout_ref, lng_ref, lnb_ref) = ins
    qk_ref, ct_ref, n_ref, m_ref, hb_ref = scratch
    L = ML_ROWS
    HALO = V7X_SUBLANES

    @pl.when(pl.program_id(0) == 0)
    def _init():
        qk_ref[0:HALO, :] = jnp.zeros((HALO, 2 * ML_QK), F32)
        ct_ref[...] = jnp.zeros_like(ct_ref)
        n_ref[...] = jnp.zeros_like(n_ref)
        m_ref[...] = jnp.zeros_like(m_ref)

    _run_casts(cast_in, cast_out)
    x = x_ref[...]
    xb = x.astype(BF16)

    for c in range(2 * ML_QK // WCHUNK):
        cols = slice(c * WCHUNK, (c + 1) * WCHUNK)
        qk_ref[HALO:HALO + L, cols] = _dot(xb, win_ref[c]) + bin_ref[:, cols]

    g_col = _dot(xb, wgc_ref[...]) + bgc_ref[...]
    g_row = _dot_nt(wgr_ref[...], xb) + bgr_ref[...]
    ri = lax.broadcasted_iota(jnp.int32, (L, L), 0)
    ci = lax.broadcasted_iota(jnp.int32, (L, L), 1)
    causal = ri >= ci
    lower = causal.astype(BF16)
    upper = (ri <= ci).astype(BF16)
    fh, fm, fl = _split3(_log_sigmoid(g_col))
    bcum_col = _dot(lower, fh) + _dot(lower, fm) + _dot(lower, fl)
    fh, fm, fl = _split3(_log_sigmoid(g_row))
    bcum_row = _dot(fh, upper) + _dot(fm, upper) + _dot(fl, upper)

    for h in range(ML_HEADS):
        qlo = h * ML_DQK
        klo = ML_QK + h * ML_DQK
        q = jnp.zeros((L, ML_DQK), F32)
        k = jnp.zeros((L, ML_DQK), F32)
        for j in range(ML_CONV):
            rows = pl.ds(HALO - (ML_CONV - 1) + j, L)
            q = q + conv_ref[j:j + 1, qlo:qlo + ML_DQK] * qk_ref[rows, qlo:qlo + ML_DQK]
            k = k + conv_ref[j:j + 1, klo:klo + ML_DQK] * qk_ref[rows, klo:klo + ML_DQK]
        q = jax.nn.silu(q)
        k = jax.nn.silu(k) * (ML_DQK ** -0.5)
        qb = q.astype(BF16)
        kb = k.astype(BF16)
        vlo = 2 * ML_QK + h * ML_DV
        v = _dot(xb, win_ref[vlo // WCHUNK]) + bin_ref[:, vlo:vlo + ML_DV]
        vb = v.astype(BF16)

        i_col = g_col[:, h:h + 1]
        b_col = bcum_col[:, ML_HEADS + h:ML_HEADS + h + 1]
        i_row = g_row[h:h + 1, :]
        b_row = bcum_row[ML_HEADS + h:ML_HEADS + h + 1, :]
        m_prev = m_ref[h][:, 0:1]
        ct = ct_ref[h]
        n_prev = n_ref[h]

        d_log = jnp.where(causal, b_col - b_row + i_row, -jnp.inf)
        inter = b_col + m_prev
        m_t = jnp.maximum(inter, jnp.max(d_log, axis=-1, keepdims=True))
        s_mat = _dot_nt(qb, kb) * jnp.exp(d_log - m_t)
        w_inter = jnp.exp(inter - m_t)
        num = _dot(s_mat.astype(BF16), vb) + w_inter * _dot(qb, ct.astype(BF16))
        den = jnp.sum(s_mat, axis=-1, keepdims=True) + w_inter * jnp.sum(q * n_prev, axis=-1, keepdims=True)
        hh = num / jnp.maximum(jnp.abs(den), jnp.exp(-m_t))

        b_last = b_col[L - 1:L, :]
        w_log = b_last - b_col + i_col
        m_new = jnp.maximum(b_last + m_prev, jnp.max(w_log, axis=0, keepdims=True))
        decay = jnp.exp(b_last + m_prev - m_new)
        kw = k * jnp.exp(w_log - m_new)
        ct_ref[h] = decay * ct + _dot_tn(kw.astype(BF16), vb)
        n_ref[h] = decay * n_prev + jnp.sum(kw, axis=0, keepdims=True)
        m_ref[h] = jnp.broadcast_to(m_new, m_ref.shape[1:])

        mu = jnp.mean(hh, axis=-1, keepdims=True)
        hc = hh - mu
        var = jnp.mean(hc * hc, axis=-1, keepdims=True)
        hn = hc * lax.rsqrt(var + LN_EPS) * ng_ref[:, h * ML_DV:(h + 1) * ML_DV]
        zlo = 2 * ML_QK + ML_INNER + h * ML_DV
        z = _dot(xb, win_ref[zlo // WCHUNK]) + bin_ref[:, zlo:zlo + ML_DV]
        hb_ref[:, h * ML_DV:(h + 1) * ML_DV] = (hn * jax.nn.sigmoid(z)).astype(BF16)

    qk_ref[0:HALO, :] = qk_ref[L:L + HALO, :]

    hb = hb_ref[...]
    y = jnp.concatenate([_dot(hb, wout_ref[0]), _dot(hb, wout_ref[1])], axis=1)
    o_ref[...] = _layer_norm(DN_ALPHA * x + y, lng_ref[1:2, :], lnb_ref[1:2, :])


def _mlstm_call(x, w_in, b_in, wg_col, wg_row, bg_col, bg_row, conv_w, norm_g, w_out, ln_g, ln_b, layer, j,
                cast_weights):
    S, D = x.shape
    L = ML_ROWS
    main_specs = [
        _resident((ML_MAIN // WCHUNK, D, WCHUNK), (0, 0, 0)),
        _resident((None, 1, ML_MAIN), (j, 0, 0)),
        _resident((None, D, 2 * ML_HEADS), (j, 0, 0)),
        _resident((None, 4 * ML_HEADS, D), (j, 0, 0)),
        _resident((None, 1, 2 * ML_HEADS), (j, 0, 0)),
        _resident((None, 4 * ML_HEADS, 1), (j, 0, 0)),
        _resident((None, ML_CONV, 2 * ML_QK), (j, 0, 0)),
        _resident((None, 1, ML_INNER), (j, 0, 0)),
        _resident((2, ML_INNER, D // 2), (0, 0, 0)),
        _resident((None, 4, D), (layer, 0, 0)),
        _resident((None, 4, D), (layer, 0, 0)),
    ]
    main_args = [w_in, b_in, wg_col, wg_row, bg_col, bg_row, conv_w, norm_g, w_out, ln_g, ln_b]
    scratch = [
        pltpu.VMEM((V7X_SUBLANES + L, 2 * ML_QK), F32),
        pltpu.VMEM((ML_HEADS, ML_DQK, ML_DV), F32),
        pltpu.VMEM((ML_HEADS, 1, ML_DQK), F32),
        pltpu.VMEM((ML_HEADS, 1, 128), F32),
        pltpu.VMEM((L, ML_INNER), BF16),
    ]
    return _call(_mlstm_kernel, "mlstm", x, pl.BlockSpec((L, D), lambda t: (t, 0)),
                 main_specs, main_args, cast_weights, scratch)


def _gmlp_kernel(n_in, n_cast, *refs):
    ins, cast_in, o_ref, cast_out, (v_ref, u_ref, y_ref) = _split_refs(refs, n_in, n_cast)
    x_ref, win_ref, bin_ref, vng_ref, vnb_ref, ws_ref, bst_ref, wout_ref, lng_ref, lnb_ref = ins
    _run_casts(cast_in, cast_out)
    xb = x_ref[...].astype(BF16)
    for g in range(GM_GROUPS):
        lo = g * GM_GDIM
        vlo = GM_WIDTH + lo
        v_ref[:, lo:lo + GM_GDIM] = _gelu(_dot(xb, win_ref[:, vlo:vlo + GM_GDIM]) + bin_ref[:, vlo:vlo + GM_GDIM])
    for g in range(GM_GROUPS):
        lo = g * GM_GDIM
        u_ref[:, lo:lo + GM_GDIM] = _gelu(_dot(xb, win_ref[:, lo:lo + GM_GDIM]) + bin_ref[:, lo:lo + GM_GDIM])
    v = v_ref[...]
    mu = jnp.mean(v, axis=-1, keepdims=True)
    vc = v - mu
    rstd = lax.rsqrt(jnp.mean(vc * vc, axis=-1, keepdims=True) + LN_EPS)

    blk_r = lax.broadcasted_iota(jnp.int32, (GM_BLOCK, GM_BLOCK), 0) // CHUNK
    blk_c = lax.broadcasted_iota(jnp.int32, (GM_BLOCK, GM_BLOCK), 1) // CHUNK
    chunk_causal = blk_r >= blk_c
    for g in range(GM_GROUPS):
        lo = g * GM_GDIM
        vn = ((v_ref[:, lo:lo + GM_GDIM] - mu) * rstd * vng_ref[:, lo:lo + GM_GDIM]
              + vnb_ref[:, lo:lo + GM_GDIM]).astype(BF16)
        wsm = jnp.where(chunk_causal, ws_ref[g], 0.0).astype(BF16)
        bias = bst_ref[:, g:g + 1]
        for nb in range(GM_ROWS // GM_BLOCK):
            r0 = nb * GM_BLOCK
            s = _dot(wsm, vn[r0:r0 + GM_BLOCK, :]) + bias
            y_ref[r0:r0 + GM_BLOCK, lo:lo + GM_GDIM] = (u_ref[r0:r0 + GM_BLOCK, lo:lo + GM_GDIM] * s).astype(BF16)
    for r in range(GM_ROWS // GM_SUB):
        rs = slice(r * GM_SUB, (r + 1) * GM_SUB)
        y = _dot(y_ref[rs, :], wout_ref[...])
        o_ref[rs, :] = _layer_norm(DN_ALPHA * x_ref[rs, :] + y, lng_ref[1:2, :], lnb_ref[1:2, :])


def _gmlp_call(x, w_in, b_in, vn_g, vn_b, ws, bs_t, w_out, ln_g, ln_b, layer, j, cast_weights):
    S, D = x.shape
    main_specs = [
        _resident((D, 2 * GM_WIDTH), (0, 0)),
        _resident((None, 1, 2 * GM_WIDTH), (j, 0, 0)),
        _resident((None, 1, GM_WIDTH), (j, 0, 0)),
        _resident((None, 1, GM_WIDTH), (j, 0, 0)),
        _resident((None, GM_GROUPS, GM_BLOCK, GM_BLOCK), (j, 0, 0, 0)),
        _resident((None, GM_BLOCK, GM_GROUPS), (j, 0, 0)),
        _resident((GM_WIDTH, D), (0, 0)),
        _resident((None, 4, D), (layer, 0, 0)),
        _resident((None, 4, D), (layer, 0, 0)),
    ]
    main_args = [w_in, b_in, vn_g, vn_b, ws, bs_t, w_out, ln_g, ln_b]
    scratch = [
        pltpu.VMEM((GM_ROWS, GM_WIDTH), F32),
        pltpu.VMEM((GM_ROWS, GM_WIDTH), F32),
        pltpu.VMEM((GM_ROWS, GM_WIDTH), BF16),
    ]
    return _call(_gmlp_kernel, "gmlp", x, pl.BlockSpec((GM_ROWS, D), lambda t: (t, 0)),
                 main_specs, main_args, cast_weights, scratch)


def kernel(x, p, ln_g, ln_b, ffn1_wgu, ffn1_wd, ffn2_wgu, ffn2_wd, ml_w_in, ml_b_in, ml_conv, ml_norm_g, ml_w_out, gm_w_in, gm_b_in, gm_vn_g, gm_vn_b, gm_ws, gm_bs, gm_w_out, ple_wp, ple_wg, ple_bg):
    B, S, D = x.shape
    assert (B, S, D) == (1, SEQ, D_MODEL)
    xs = x.reshape(S, D)
    ps = p.reshape(DEPTH, S, PLE_DIM)

    ple_bg_r = ple_bg[:, None, :]
    ml_b_main = ml_b_in[:, None, :ML_MAIN]
    ml_wg_col = ml_w_in[:, :, ML_MAIN:].astype(BF16)
    ml_wg_row = jnp.pad(jnp.swapaxes(ml_w_in[:, :, ML_MAIN:], 1, 2),
                        ((0, 0), (0, 2 * ML_HEADS), (0, 0))).astype(BF16)
    ml_bg_col = ml_b_in[:, None, ML_MAIN:]
    ml_bg_row = jnp.pad(ml_b_in[:, ML_MAIN:], ((0, 0), (0, 2 * ML_HEADS)))[:, :, None]
    ml_norm_g_r = ml_norm_g[:, None, :]
    ml_w_in_t = jnp.swapaxes(ml_w_in, 1, 2)
    gm_b_in_r = gm_b_in[:, None, :]
    gm_vn_g_r, gm_vn_b_r = gm_vn_g[:, None, :], gm_vn_b[:, None, :]
    gm_bs_t = jnp.swapaxes(gm_bs, 1, 2)

    ffn1_w = (ffn1_wgu[0].astype(BF16), ffn1_wd[0].astype(BF16))
    for i in range(DEPTH):
        j = i // 2
        if i % 2 == 0:
            mixer_f32 = [("transposed", ml_w_in_t, j), ("halves", ml_w_out, j)]
        else:
            mixer_f32 = [("plain", gm_w_in, j), ("plain", gm_w_out, j)]
        ffn2_rest_f32 = [("plain", ffn2_wd, i), ("plain", ple_wg, i), ("plain", ple_wp, i)]
        xs, cast_w = _ffn_call(xs, *ffn1_w, ln_g, ln_b, i, 0, mixer_f32 + ffn2_rest_f32)
        mixer_w, ffn2_rest_w = cast_w[:2], cast_w[2:]
        ffn2_f32 = [("plain", ffn2_wgu, i)]
        if i % 2 == 0:
            xs, ffn2_w = _mlstm_call(xs, mixer_w[0], ml_b_main, ml_wg_col, ml_wg_row, ml_bg_col, ml_bg_row,
                                     ml_conv, ml_norm_g_r, mixer_w[1], ln_g, ln_b, i, j, ffn2_f32)
        else:
            xs, ffn2_w = _gmlp_call(xs, mixer_w[0], gm_b_in_r, gm_vn_g_r, gm_vn_b_r, gm_ws, gm_bs_t,
                                    mixer_w[1], ln_g, ln_b, i, j, ffn2_f32)
        next_f32 = [("plain", ffn1_wgu, i + 1), ("plain", ffn1_wd, i + 1)] if i + 1 < DEPTH else []
        xs, ffn1_w = _ffn_call(xs, ffn2_w[0], ffn2_rest_w[0], ln_g, ln_b, i, 2, next_f32,
                               ple=(ps, ffn2_rest_w[1], ple_bg_r, ffn2_rest_w[2]))
    return xs.reshape(B, S, D)
```

```python
import functools

import jax
import jax.numpy as jnp
from jax import lax
from jax.experimental import pallas as pl
from jax.experimental.pallas import tpu as pltpu

D_MODEL = 1024
SEQ = 16384
DEPTH = 4
CHUNK = 64
PLE_DIM = 256
FFN_HIDDEN = 2816
ML_HEADS = 4
ML_DQK = 256
ML_DV = 512
ML_INNER = ML_HEADS * ML_DV
ML_QK = ML_HEADS * ML_DQK
ML_MAIN = 2 * ML_QK + 2 * ML_INNER
ML_CONV = 4
GM_BLOCK = 128
GM_GROUPS = 8
GM_WIDTH = 2 * D_MODEL
GM_GDIM = GM_WIDTH // GM_GROUPS
DN_ALPHA = (2.0 * DEPTH) ** 0.25
LN_EPS = 1e-5

V7X_MXU_DIM = 256
V7X_SUBLANES = 8
V7X_LANES = 128
V7X_BF16_ROWS = 16
V7X_VMEM_BYTES = 64 * 1024 * 1024
VMEM_LIMIT_BYTES = V7X_VMEM_BYTES * 7 // 8

FFN_ROWS = 1024
FFN_PLE_ROWS = 1024
FFN_COLS = V7X_MXU_DIM
FFN_SUB = 256
ML_ROWS = 256
ML_TILE = 512
GM_ROWS = 512
GM_SUB = 256
WCHUNK = 512

F32 = jnp.float32
BF16 = jnp.bfloat16


def _dot(a, b):
    return jnp.dot(a, b, preferred_element_type=F32)


def _dot_nt(a, b):
    return lax.dot_general(a, b, (((1,), (1,)), ((), ())), preferred_element_type=F32)


def _dot_tn(a, b):
    return lax.dot_general(a, b, (((0,), (0,)), ((), ())), preferred_element_type=F32)


def _layer_norm(z, g, b):
    mu = jnp.mean(z, axis=-1, keepdims=True)
    zc = z - mu
    var = jnp.mean(zc * zc, axis=-1, keepdims=True)
    return zc * lax.rsqrt(var + LN_EPS) * g + b


def _split3(a):
    hi = a.astype(BF16)
    r1 = a - hi.astype(F32)
    mid = r1.astype(BF16)
    lo = (r1 - mid.astype(F32)).astype(BF16)
    return hi, mid, lo


def _gelu(x):
    return 0.5 * x * (1.0 + lax.erf(x * (2.0 ** -0.5)))


def _log_sigmoid(x):
    return jnp.minimum(x, 0.0) - jnp.log1p(jnp.exp(-jnp.abs(x)))


def _resident(shape, index):
    return pl.BlockSpec(shape, lambda t: index, pipeline_mode=pl.Buffered(1))


def _cast_plan(weights, steps):
    in_specs, out_specs, out_shapes, args = [], [], [], []
    for kind, w, layer in weights:
        args.append(w)
        if kind == "transposed":
            _, N, K = w.shape
            last = N // WCHUNK - 1
            assert steps > last
            in_specs.append(pl.BlockSpec((None, WCHUNK, K),
                                         lambda t, layer=layer, last=last: (layer, jnp.minimum(t, last), 0)))
            out_specs.append(pl.BlockSpec((None, K, WCHUNK), lambda t, last=last: (jnp.minimum(t, last), 0, 0)))
            out_shapes.append(jax.ShapeDtypeStruct((last + 1, K, WCHUNK), BF16))
            continue
        _, K, N = w.shape
        hold = 1
        while K % (steps // hold) or (K // (steps // hold)) % V7X_BF16_ROWS:
            hold *= 2
        rows = K // (steps // hold)
        in_specs.append(pl.BlockSpec((None, rows, N), lambda t, layer=layer, hold=hold: (layer, t // hold, 0)))
        if kind == "halves":
            out_specs.append(pl.BlockSpec((2, rows, N // 2), lambda t, hold=hold: (0, t // hold, 0)))
            out_shapes.append(jax.ShapeDtypeStruct((2, K, N // 2), BF16))
        else:
            out_specs.append(pl.BlockSpec((rows, N), lambda t, hold=hold: (t // hold, 0)))
            out_shapes.append(jax.ShapeDtypeStruct((K, N), BF16))
    return in_specs, out_specs, out_shapes, args


def _split_refs(refs, n_in, n_cast):
    a, b, c = n_in, n_in + n_cast, n_in + 2 * n_cast + 1
    return refs[:a], refs[a:b], refs[b], refs[b + 1:c], refs[c:]


def _run_casts(cast_in, cast_out):
    for src, dst in zip(cast_in, cast_out):
        if len(dst.shape) == 3:
            half = dst.shape[2]
            dst[0] = src[:, :half].astype(BF16)
            dst[1] = src[:, half:].astype(BF16)
        elif dst.shape != src.shape:
            dst[...] = src[...].T.astype(BF16)
        else:
            dst[...] = src[...].astype(BF16)


def _call(kernel_fn, name, x, row_spec, main_specs, main_args, cast_weights, scratch_shapes):
    S, D = x.shape
    steps = S // row_spec.block_shape[0]
    c_in, c_out, c_shapes, c_args = _cast_plan(cast_weights, steps)
    outs = pl.pallas_call(
        functools.partial(kernel_fn, 1 + len(main_args), len(c_args)),
        grid=(steps,),
        in_specs=[row_spec] + main_specs + c_in,
        out_specs=[row_spec] + c_out,
        out_shape=[jax.ShapeDtypeStruct((S, D), F32)] + c_shapes,
        scratch_shapes=scratch_shapes,
        compiler_params=pltpu.CompilerParams(
            dimension_semantics=("arbitrary",), vmem_limit_bytes=VMEM_LIMIT_BYTES),
        name=name,
    )(x, *main_args, *c_args)
    return outs[0], outs[1:]


def _ffn_kernel(with_ple, ln_row, n_in, n_cast, *refs):
    ins, cast_in, o_ref, cast_out, (a_ref,) = _split_refs(refs, n_in, n_cast)
    if with_ple:
        x_ref, wgu_ref, wd_ref, lng_ref, lnb_ref, p_ref, wpg_ref, bpg_ref, wpp_ref = ins
    else:
        x_ref, wgu_ref, wd_ref, lng_ref, lnb_ref = ins
    _run_casts(cast_in, cast_out)
    xb = x_ref[...].astype(BF16)
    for c in range(FFN_HIDDEN // FFN_COLS):
        lo = c * FFN_COLS
        g = _dot(xb, wgu_ref[:, lo:lo + FFN_COLS])
        u = _dot(xb, wgu_ref[:, FFN_HIDDEN + lo:FFN_HIDDEN + lo + FFN_COLS])
        a_ref[:, lo:lo + FFN_COLS] = (jax.nn.silu(g) * u).astype(BF16)
    n_sub = a_ref.shape[0] // FFN_SUB

    def down(r):
        rs = slice(r * FFN_SUB, (r + 1) * FFN_SUB)
        y = _dot(a_ref[rs, :], wd_ref[...])
        return _layer_norm(DN_ALPHA * x_ref[rs, :] + 0.5 * y,
                           lng_ref[ln_row:ln_row + 1, :], lnb_ref[ln_row:ln_row + 1, :])

    x1 = down(0)
    for r in range(n_sub):
        rs = slice(r * FFN_SUB, (r + 1) * FFN_SUB)
        x1_next = down(r + 1) if r + 1 < n_sub else None
        if with_ple:
            gate = jax.nn.sigmoid(_dot(x1.astype(BF16), wpg_ref[...]) + bpg_ref[...])
            emb = _dot(p_ref[rs, :].astype(BF16), wpp_ref[...])
            x1 = _layer_norm(DN_ALPHA * x1 + gate * emb, lng_ref[3:4, :], lnb_ref[3:4, :])
        o_ref[rs, :] = x1
        x1 = x1_next


def _ffn_call(x, wgu, wd, ln_g, ln_b, layer, ln_row, cast_weights, ple=None):
    S, D = x.shape
    F = FFN_HIDDEN
    tm = FFN_ROWS if ple is None else FFN_PLE_ROWS
    main_specs = [
        _resident((D, 2 * F), (0, 0)),
        _resident((F, D), (0, 0)),
        _resident((None, 4, D), (layer, 0, 0)),
        _resident((None, 4, D), (layer, 0, 0)),
    ]
    main_args = [wgu, wd, ln_g, ln_b]
    if ple is not None:
        p, wpg, bpg, wpp = ple
        main_specs += [
            pl.BlockSpec((None, tm, PLE_DIM), lambda t: (layer, t, 0)),
            _resident((D, D), (0, 0)),
            _resident((None, 1, D), (layer, 0, 0)),
            _resident((PLE_DIM, D), (0, 0)),
        ]
        main_args += [p, wpg, bpg, wpp]
    return _call(functools.partial(_ffn_kernel, ple is not None, ln_row),
                 "ffn_ple" if ple is not None else "ffn",
                 x, pl.BlockSpec((tm, D), lambda t: (t, 0)), main_specs, main_args, cast_weights,
                 [pltpu.VMEM((tm, F), BF16)])


def _conv_silu(qk_ref, act_ref, conv_ref, row0, col0, scale):
    L = ML_ROWS
    HALO = V7X_SUBLANES
    groups = L // V7X_SUBLANES
    outs = []
    for b in range(ML_DQK // V7X_LANES):
        blk = col0 // V7X_LANES + b
        acc = None
        for j in range(ML_CONV):
            start = row0 + HALO - (ML_CONV - 1) + j
            tap = jnp.concatenate(
                [qk_ref[blk, pl.ds(start + s, groups, stride=V7X_SUBLANES), :] for s in range(V7X_SUBLANES)], axis=0)
            term = conv_ref[j:j + 1, blk * V7X_LANES:(blk + 1) * V7X_LANES] * tap
            acc = term if acc is None else acc + term
        act = jax.nn.silu(acc) * scale if scale != 1.0 else jax.nn.silu(acc)
        for s in range(V7X_SUBLANES):
            act_ref[blk, pl.ds(row0 + s, groups, stride=V7X_SUBLANES), :] = act[s * groups:(s + 1) * groups, :]
        outs.append(act_ref[blk, row0:row0 + L, :])
    return jnp.concatenate(outs, axis=1)


def _mlstm_kernel(n_in, n_cast, *refs):
    ins, cast_in, o_ref, cast_out, scratch = _split_refs(refs, n_in, n_cast)
    (x_ref, win_ref, bin_ref, wgc_ref, wgr_ref, bgc_ref, bgr_ref, conv_ref, ng_ref,
     wout_ref, lng_ref, lnb_ref) = ins
    qk_ref, act_ref, ct_ref, n_ref, m_ref, hb_ref = scratch
    L = ML_ROWS
    HALO = V7X_SUBLANES
    n_chunks = x_ref.shape[0] // L
    n_blk = WCHUNK // V7X_LANES

    @pl.when(pl.program_id(0) == 0)
    def _init():
        qk_ref[:, 0:HALO, :] = jnp.zeros((2 * ML_QK // V7X_LANES, HALO, V7X_LANES), F32)
        ct_ref[...] = jnp.zeros_like(ct_ref)
        n_ref[...] = jnp.zeros_like(n_ref)
        m_ref[...] = jnp.zeros_like(m_ref)

    _run_casts(cast_in, cast_out)
    xbs = [x_ref[c * L:(c + 1) * L, :].astype(BF16) for c in range(n_chunks)]

    ri = lax.broadcasted_iota(jnp.int32, (L, L), 0)
    ci = lax.broadcasted_iota(jnp.int32, (L, L), 1)
    causal = ri >= ci
    lower = causal.astype(BF16)
    upper = (ri <= ci).astype(BF16)

    def qk_piece(c, piece):
        pre = _dot(xbs[c], win_ref[piece]) + bin_ref[:, piece * WCHUNK:(piece + 1) * WCHUNK]
        for b in range(n_blk):
            qk_ref[piece * n_blk + b, HALO + c * L:HALO + (c + 1) * L, :] = pre[:, b * V7X_LANES:(b + 1) * V7X_LANES]

    def gates(c):
        g_col = _dot(xbs[c], wgc_ref[...]) + bgc_ref[...]
        g_row = _dot_nt(wgr_ref[...], xbs[c]) + bgr_ref[...]
        fh, fm, fl = _split3(_log_sigmoid(g_col))
        bcum_col = _dot(lower, fh) + _dot(lower, fm) + _dot(lower, fl)
        fh, fm, fl = _split3(_log_sigmoid(g_row))
        bcum_row = _dot(fh, upper) + _dot(fm, upper) + _dot(fl, upper)
        return g_col, g_row, bcum_col, bcum_row

    def head(c, h, gate_vals):
        g_col, g_row, bcum_col, bcum_row = gate_vals
        xb = xbs[c]
        q = _conv_silu(qk_ref, act_ref, conv_ref, c * L, h * ML_DQK, 1.0)
        k = _conv_silu(qk_ref, act_ref, conv_ref, c * L, ML_QK + h * ML_DQK, ML_DQK ** -0.5)
        qb = q.astype(BF16)
        kb = k.astype(BF16)
        vlo = 2 * ML_QK + h * ML_DV
        v = _dot(xb, win_ref[vlo // WCHUNK]) + bin_ref[:, vlo:vlo + ML_DV]
        vb = v.astype(BF16)

        i_col = g_col[:, h:h + 1]
        b_col = bcum_col[:, ML_HEADS + h:ML_HEADS + h + 1]
        i_row = g_row[h:h + 1, :]
        b_row = bcum_row[ML_HEADS + h:ML_HEADS + h + 1, :]
        m_prev = m_ref[h][:, 0:1]
        ct = ct_ref[h]
        n_prev = n_ref[h]

        d_log = jnp.where(causal, b_col - b_row + i_row, -jnp.inf)
        inter = b_col + m_prev
        m_t = jnp.maximum(inter, jnp.max(d_log, axis=-1, keepdims=True))
        s_mat = _dot_nt(qb, kb) * jnp.exp(d_log - m_t)
        w_inter = jnp.exp(inter - m_t)
        num = _dot(s_mat.astype(BF16), vb) + w_inter * _dot(qb, ct.astype(BF16))
        den = jnp.sum(s_mat, axis=-1, keepdims=True) + w_inter * jnp.sum(q * n_prev, axis=-1, keepdims=True)
        hh = num / jnp.maximum(jnp.abs(den), jnp.exp(-m_t))

        b_last = b_col[L - 1:L, :]
        w_log = b_last - b_col + i_col
        m_new = jnp.maximum(b_last + m_prev, jnp.max(w_log, axis=0, keepdims=True))
        decay = jnp.exp(b_last + m_prev - m_new)
        kw = k * jnp.exp(w_log - m_new)
        ct_ref[h] = decay * ct + _dot_tn(kw.astype(BF16), vb)
        n_ref[h] = decay * n_prev + jnp.sum(kw, axis=0, keepdims=True)
        m_ref[h] = jnp.broadcast_to(m_new, m_ref.shape[1:])

        mu = jnp.mean(hh, axis=-1, keepdims=True)
        hc = hh - mu
        var = jnp.mean(hc * hc, axis=-1, keepdims=True)
        hn = hc * lax.rsqrt(var + LN_EPS) * ng_ref[:, h * ML_DV:(h + 1) * ML_DV]
        zlo = 2 * ML_QK + ML_INNER + h * ML_DV
        z = _dot(xb, win_ref[zlo // WCHUNK]) + bin_ref[:, zlo:zlo + ML_DV]
        hb_ref[c * L:(c + 1) * L, h * ML_DV:(h + 1) * ML_DV] = (hn * jax.nn.sigmoid(z)).astype(BF16)

    def out_half(c, half):
        return _dot(hb_ref[c * L:(c + 1) * L, :], wout_ref[half])

    def finish(c, halves):
        rows = slice(c * L, (c + 1) * L)
        y = jnp.concatenate(halves, axis=1)
        o_ref[rows, :] = _layer_norm(DN_ALPHA * x_ref[rows, :] + y, lng_ref[1:2, :], lnb_ref[1:2, :])

    n_pieces = 2 * ML_QK // WCHUNK
    for piece in range(n_pieces):
        qk_piece(0, piece)
    gate_vals = [gates(c) for c in range(n_chunks)]
    for c in range(n_chunks):
        halves = []
        for h in range(ML_HEADS):
            head(c, h, gate_vals[c])
            if c + 1 < n_chunks:
                qk_piece(c + 1, h)
            if c > 0 and h < 2:
                halves.append(out_half(c - 1, h))
                if h == 1:
                    finish(c - 1, halves)
    finish(n_chunks - 1, [out_half(n_chunks - 1, 0), out_half(n_chunks - 1, 1)])

    qk_ref[:, 0:HALO, :] = qk_ref[:, n_chunks * L:n_chunks * L + HALO, :]


def _mlstm_call(x, w_in, b_in, wg_col, wg_row, bg_col, bg_row, conv_w, norm_g, w_out, ln_g, ln_b, layer, j,
                cast_weights):
    S, D = x.shape
    L = ML_ROWS
    main_specs = [
        _resident((ML_MAIN // WCHUNK, D, WCHUNK), (0, 0, 0)),
        _resident((None, 1, ML_MAIN), (j, 0, 0)),
        _resident((None, D, 2 * ML_HEADS), (j, 0, 0)),
        _resident((None, 4 * ML_HEADS, D), (j, 0, 0)),
        _resident((None, 1, 2 * ML_HEADS), (j, 0, 0)),
        _resident((None, 4 * ML_HEADS, 1), (j, 0, 0)),
        _resident((None, ML_CONV, 2 * ML_QK), (j, 0, 0)),
        _resident((None, 1, ML_INNER), (j, 0, 0)),
        _resident((2, ML_INNER, D // 2), (0, 0, 0)),
        _resident((None, 4, D), (layer, 0, 0)),
        _resident((None, 4, D), (layer, 0, 0)),
    ]
    main_args = [w_in, b_in, wg_col, wg_row, bg_col, bg_row, conv_w, norm_g, w_out, ln_g, ln_b]
    scratch = [
        pltpu.VMEM((2 * ML_QK // V7X_LANES, V7X_SUBLANES + ML_TILE, V7X_LANES), F32),
        pltpu.VMEM((2 * ML_QK // V7X_LANES, ML_TILE, V7X_LANES), F32),
        pltpu.VMEM((ML_HEADS, ML_DQK, ML_DV), F32),
        pltpu.VMEM((ML_HEADS, 1, ML_DQK), F32),
        pltpu.VMEM((ML_HEADS, 1, 128), F32),
        pltpu.VMEM((ML_TILE, ML_INNER), BF16),
    ]
    return _call(_mlstm_kernel, "mlstm", x, pl.BlockSpec((ML_TILE, D), lambda t: (t, 0)),
                 main_specs, main_args, cast_weights, scratch)


def _gmlp_kernel(n_in, n_cast, *refs):
    ins, cast_in, o_ref, cast_out, (v_ref, u_ref, y_ref) = _split_refs(refs, n_in, n_cast)
    x_ref, win_ref, bin_ref, vng_ref, vnb_ref, ws_ref, bst_ref, wout_ref, lng_ref, lnb_ref = ins
    _run_casts(cast_in, cast_out)
    xb = x_ref[...].astype(BF16)
    for g in range(GM_GROUPS):
        lo = g * GM_GDIM
        vlo = GM_WIDTH + lo
        v_ref[:, lo:lo + GM_GDIM] = _gelu(_dot(xb, win_ref[:, vlo:vlo + GM_GDIM]) + bin_ref[:, vlo:vlo + GM_GDIM])
    for g in range(GM_GROUPS):
        lo = g * GM_GDIM
        u_ref[:, lo:lo + GM_GDIM] = _gelu(_dot(xb, win_ref[:, lo:lo + GM_GDIM]) + bin_ref[:, lo:lo + GM_GDIM])
    v = v_ref[...]
    mu = jnp.mean(v, axis=-1, keepdims=True)
    vc = v - mu
    rstd = lax.rsqrt(jnp.mean(vc * vc, axis=-1, keepdims=True) + LN_EPS)

    blk_r = lax.broadcasted_iota(jnp.int32, (GM_BLOCK, GM_BLOCK), 0) // CHUNK
    blk_c = lax.broadcasted_iota(jnp.int32, (GM_BLOCK, GM_BLOCK), 1) // CHUNK
    chunk_causal = blk_r >= blk_c
    for g in range(GM_GROUPS):
        lo = g * GM_GDIM
        vn = ((v_ref[:, lo:lo + GM_GDIM] - mu) * rstd * vng_ref[:, lo:lo + GM_GDIM]
              + vnb_ref[:, lo:lo + GM_GDIM]).astype(BF16)
        wsm = jnp.where(chunk_causal, ws_ref[g], 0.0).astype(BF16)
        bias = bst_ref[:, g:g + 1]
        for nb in range(GM_ROWS // GM_BLOCK):
            r0 = nb * GM_BLOCK
            s = _dot(wsm, vn[r0:r0 + GM_BLOCK, :]) + bias
            y_ref[r0:r0 + GM_BLOCK, lo:lo + GM_GDIM] = (u_ref[r0:r0 + GM_BLOCK, lo:lo + GM_GDIM] * s).astype(BF16)
    for r in range(GM_ROWS // GM_SUB):
        rs = slice(r * GM_SUB, (r + 1) * GM_SUB)
        y = _dot(y_ref[rs, :], wout_ref[...])
        o_ref[rs, :] = _layer_norm(DN_ALPHA * x_ref[rs, :] + y, lng_ref[1:2, :], lnb_ref[1:2, :])


def _gmlp_call(x, w_in, b_in, vn_g, vn_b, ws, bs_t, w_out, ln_g, ln_b, layer, j, cast_weights):
    S, D = x.shape
    main_specs = [
        _resident((D, 2 * GM_WIDTH), (0, 0)),
        _resident((None, 1, 2 * GM_WIDTH), (j, 0, 0)),
        _resident((None, 1, GM_WIDTH), (j, 0, 0)),
        _resident((None, 1, GM_WIDTH), (j, 0, 0)),
        _resident((None, GM_GROUPS, GM_BLOCK, GM_BLOCK), (j, 0, 0, 0)),
        _resident((None, GM_BLOCK, GM_GROUPS), (j, 0, 0)),
        _resident((GM_WIDTH, D), (0, 0)),
        _resident((None, 4, D), (layer, 0, 0)),
        _resident((None, 4, D), (layer, 0, 0)),
    ]
    main_args = [w_in, b_in, vn_g, vn_b, ws, bs_t, w_out, ln_g, ln_b]
    scratch = [
        pltpu.VMEM((GM_ROWS, GM_WIDTH), F32),
        pltpu.VMEM((GM_ROWS, GM_WIDTH), F32),
        pltpu.VMEM((GM_ROWS, GM_WIDTH), BF16),
    ]
    return _call(_gmlp_kernel, "gmlp", x, pl.BlockSpec((GM_ROWS, D), lambda t: (t, 0)),
                 main_specs, main_args, cast_weights, scratch)


def kernel(x, p, ln_g, ln_b, ffn1_wgu, ffn1_wd, ffn2_wgu, ffn2_wd, ml_w_in, ml_b_in, ml_conv, ml_norm_g, ml_w_out, gm_w_in, gm_b_in, gm_vn_g, gm_vn_b, gm_ws, gm_bs, gm_w_out, ple_wp, ple_wg, ple_bg):
    B, S, D = x.shape
    assert (B, S, D) == (1, SEQ, D_MODEL)
    xs = x.reshape(S, D)
    ps = p.reshape(DEPTH, S, PLE_DIM)

    ple_bg_r = ple_bg[:, None, :]
    ml_b_main = ml_b_in[:, None, :ML_MAIN]
    ml_wg_col = ml_w_in[:, :, ML_MAIN:].astype(BF16)
    ml_wg_row = jnp.pad(jnp.swapaxes(ml_w_in[:, :, ML_MAIN:], 1, 2),
                        ((0, 0), (0, 2 * ML_HEADS), (0, 0))).astype(BF16)
    ml_bg_col = ml_b_in[:, None, ML_MAIN:]
    ml_bg_row = jnp.pad(ml_b_in[:, ML_MAIN:], ((0, 0), (0, 2 * ML_HEADS)))[:, :, None]
    ml_norm_g_r = ml_norm_g[:, None, :]
    ml_w_in_t = jnp.swapaxes(ml_w_in, 1, 2)
    gm_b_in_r = gm_b_in[:, None, :]
    gm_vn_g_r, gm_vn_b_r = gm_vn_g[:, None, :], gm_vn_b[:, None, :]
    gm_bs_t = jnp.swapaxes(gm_bs, 1, 2)

    ffn1_w = (ffn1_wgu[0].astype(BF16), ffn1_wd[0].astype(BF16))
    for i in range(DEPTH):
        j = i // 2
        if i % 2 == 0:
            mixer_f32 = [("transposed", ml_w_in_t, j), ("halves", ml_w_out, j)]
        else:
            mixer_f32 = [("plain", gm_w_in, j), ("plain", gm_w_out, j)]
        ffn2_rest_f32 = [("plain", ffn2_wd, i), ("plain", ple_wg, i), ("plain", ple_wp, i)]
        xs, cast_w = _ffn_call(xs, *ffn1_w, ln_g, ln_b, i, 0, mixer_f32 + ffn2_rest_f32)
        mixer_w, ffn2_rest_w = cast_w[:2], cast_w[2:]
        ffn2_f32 = [("plain", ffn2_wgu, i)]
        if i % 2 == 0:
            xs, ffn2_w = _mlstm_call(xs, mixer_w[0], ml_b_main, ml_wg_col, ml_wg_row, ml_bg_col, ml_bg_row,
                                     ml_conv, ml_norm_g_r, mixer_w[1], ln_g, ln_b, i, j, ffn2_f32)
        else:
            xs, ffn2_w = _gmlp_call(xs, mixer_w[0], gm_b_in_r, gm_vn_g_r, gm_vn_b_r, gm_ws, gm_bs_t,
                                    mixer_w[1], ln_g, ln_b, i, j, ffn2_f32)
        next_f32 = [("plain", ffn1_wgu, i + 1), ("plain", ffn1_wd, i + 1)] if i + 1 < DEPTH else []
        xs, ffn1_w = _ffn_call(xs, ffn2_w[0], ffn2_rest_w[0], ln_g, ln_b, i, 2, next_f32,
                               ple=(ps, ffn2_rest_w[1], ple_bg_r, ffn2_rest_w[2]))
    return xs.reshape(B, S, D)
```

```python
import functools

import jax
import jax.numpy as jnp
from jax import lax
from jax.experimental import pallas as pl
from jax.experimental.pallas import tpu as pltpu

D_MODEL = 1024
SEQ = 16384
DEPTH = 4
CHUNK = 64
PLE_DIM = 256
FFN_HIDDEN = 2816
ML_HEADS = 4
ML_DQK = 256
ML_DV = 512
ML_INNER = ML_HEADS * ML_DV
ML_QK = ML_HEADS * ML_DQK
ML_MAIN = 2 * ML_QK + 2 * ML_INNER
ML_CONV = 4
GM_BLOCK = 128
GM_GROUPS = 8
GM_WIDTH = 2 * D_MODEL
GM_GDIM = GM_WIDTH // GM_GROUPS
DN_ALPHA = (2.0 * DEPTH) ** 0.25
LN_EPS = 1e-5

V7X_MXU_DIM = 256
V7X_SUBLANES = 8
V7X_BF16_ROWS = 16
V7X_VMEM_BYTES = 64 * 1024 * 1024
VMEM_LIMIT_BYTES = V7X_VMEM_BYTES * 7 // 8

FFN_ROWS = 1024
FFN_PLE_ROWS = 1024
FFN_COLS = V7X_MXU_DIM
FFN_SUB = 256
ML_ROWS = 256
ML_TILE = 512
GM_ROWS = 1024
GM_SUB = 256
WCHUNK = 512

F32 = jnp.float32
BF16 = jnp.bfloat16


def _dot(a, b):
    return jnp.dot(a, b, preferred_element_type=F32)


def _dot_nt(a, b):
    return lax.dot_general(a, b, (((1,), (1,)), ((), ())), preferred_element_type=F32)


def _dot_tn(a, b):
    return lax.dot_general(a, b, (((0,), (0,)), ((), ())), preferred_element_type=F32)


def _layer_norm(z, g, b):
    mu = jnp.mean(z, axis=-1, keepdims=True)
    zc = z - mu
    var = jnp.mean(zc * zc, axis=-1, keepdims=True)
    return zc * lax.rsqrt(var + LN_EPS) * g + b


def _split3(a):
    hi = a.astype(BF16)
    r1 = a - hi.astype(F32)
    mid = r1.astype(BF16)
    lo = (r1 - mid.astype(F32)).astype(BF16)
    return hi, mid, lo


def _gelu(x):
    return 0.5 * x * (1.0 + lax.erf(x * (2.0 ** -0.5)))


def _log_sigmoid(x):
    return jnp.minimum(x, 0.0) - jnp.log1p(jnp.exp(-jnp.abs(x)))


def _resident(shape, index):
    return pl.BlockSpec(shape, lambda t: index, pipeline_mode=pl.Buffered(1))


def _cast_plan(weights, steps):
    in_specs, out_specs, out_shapes, args = [], [], [], []
    for kind, w, layer in weights:
        args.append(w)
        if kind == "transposed":
            _, N, K = w.shape
            last = N // WCHUNK - 1
            assert steps > last
            in_specs.append(pl.BlockSpec((None, WCHUNK, K),
                                         lambda t, layer=layer, last=last: (layer, jnp.minimum(t, last), 0)))
            out_specs.append(pl.BlockSpec((None, K, WCHUNK), lambda t, last=last: (jnp.minimum(t, last), 0, 0)))
            out_shapes.append(jax.ShapeDtypeStruct((last + 1, K, WCHUNK), BF16))
            continue
        _, K, N = w.shape
        hold = 1
        while K % (steps // hold) or (K // (steps // hold)) % V7X_BF16_ROWS:
            hold *= 2
        rows = K // (steps // hold)
        in_specs.append(pl.BlockSpec((None, rows, N), lambda t, layer=layer, hold=hold: (layer, t // hold, 0)))
        if kind == "halves":
            out_specs.append(pl.BlockSpec((2, rows, N // 2), lambda t, hold=hold: (0, t // hold, 0)))
            out_shapes.append(jax.ShapeDtypeStruct((2, K, N // 2), BF16))
        else:
            out_specs.append(pl.BlockSpec((rows, N), lambda t, hold=hold: (t // hold, 0)))
            out_shapes.append(jax.ShapeDtypeStruct((K, N), BF16))
    return in_specs, out_specs, out_shapes, args


def _split_refs(refs, n_in, n_cast):
    a, b, c = n_in, n_in + n_cast, n_in + 2 * n_cast + 1
    return refs[:a], refs[a:b], refs[b], refs[b + 1:c], refs[c:]


def _run_casts(cast_in, cast_out):
    for src, dst in zip(cast_in, cast_out):
        if len(dst.shape) == 3:
            half = dst.shape[2]
            dst[0] = src[:, :half].astype(BF16)
            dst[1] = src[:, half:].astype(BF16)
        elif dst.shape != src.shape:
            dst[...] = src[...].T.astype(BF16)
        else:
            dst[...] = src[...].astype(BF16)


def _call(kernel_fn, name, x, row_spec, main_specs, main_args, cast_weights, scratch_shapes):
    S, D = x.shape
    steps = S // row_spec.block_shape[0]
    c_in, c_out, c_shapes, c_args = _cast_plan(cast_weights, steps)
    outs = pl.pallas_call(
        functools.partial(kernel_fn, 1 + len(main_args), len(c_args)),
        grid=(steps,),
        in_specs=[row_spec] + main_specs + c_in,
        out_specs=[row_spec] + c_out,
        out_shape=[jax.ShapeDtypeStruct((S, D), F32)] + c_shapes,
        scratch_shapes=scratch_shapes,
        compiler_params=pltpu.CompilerParams(
            dimension_semantics=("arbitrary",), vmem_limit_bytes=VMEM_LIMIT_BYTES),
        name=name,
    )(x, *main_args, *c_args)
    return outs[0], outs[1:]


def _ffn_kernel(with_ple, ln_row, n_in, n_cast, *refs):
    ins, cast_in, o_ref, cast_out, (a_ref,) = _split_refs(refs, n_in, n_cast)
    if with_ple:
        x_ref, wgu_ref, wd_ref, lng_ref, lnb_ref, p_ref, wpg_ref, bpg_ref, wpp_ref = ins
    else:
        x_ref, wgu_ref, wd_ref, lng_ref, lnb_ref = ins
    _run_casts(cast_in, cast_out)
    xb = x_ref[...].astype(BF16)
    for c in range(FFN_HIDDEN // FFN_COLS):
        lo = c * FFN_COLS
        g = _dot(xb, wgu_ref[:, lo:lo + FFN_COLS])
        u = _dot(xb, wgu_ref[:, FFN_HIDDEN + lo:FFN_HIDDEN + lo + FFN_COLS])
        a_ref[:, lo:lo + FFN_COLS] = (jax.nn.silu(g) * u).astype(BF16)
    n_sub = a_ref.shape[0] // FFN_SUB

    def down(r):
        rs = slice(r * FFN_SUB, (r + 1) * FFN_SUB)
        y = _dot(a_ref[rs, :], wd_ref[...])
        return _layer_norm(DN_ALPHA * x_ref[rs, :] + 0.5 * y,
                           lng_ref[ln_row:ln_row + 1, :], lnb_ref[ln_row:ln_row + 1, :])

    x1 = down(0)
    for r in range(n_sub):
        rs = slice(r * FFN_SUB, (r + 1) * FFN_SUB)
        x1_next = down(r + 1) if r + 1 < n_sub else None
        if with_ple:
            gate = jax.nn.sigmoid(_dot(x1.astype(BF16), wpg_ref[...]) + bpg_ref[...])
            emb = _dot(p_ref[rs, :].astype(BF16), wpp_ref[...])
            x1 = _layer_norm(DN_ALPHA * x1 + gate * emb, lng_ref[3:4, :], lnb_ref[3:4, :])
        o_ref[rs, :] = x1
        x1 = x1_next


def _ffn_call(x, wgu, wd, ln_g, ln_b, layer, ln_row, cast_weights, ple=None):
    S, D = x.shape
    F = FFN_HIDDEN
    tm = FFN_ROWS if ple is None else FFN_PLE_ROWS
    main_specs = [
        _resident((D, 2 * F), (0, 0)),
        _resident((F, D), (0, 0)),
        _resident((None, 4, D), (layer, 0, 0)),
        _resident((None, 4, D), (layer, 0, 0)),
    ]
    main_args = [wgu, wd, ln_g, ln_b]
    if ple is not None:
        p, wpg, bpg, wpp = ple
        main_specs += [
            pl.BlockSpec((None, tm, PLE_DIM), lambda t: (layer, t, 0)),
            _resident((D, D), (0, 0)),
            _resident((None, 1, D), (layer, 0, 0)),
            _resident((PLE_DIM, D), (0, 0)),
        ]
        main_args += [p, wpg, bpg, wpp]
    return _call(functools.partial(_ffn_kernel, ple is not None, ln_row),
                 "ffn_ple" if ple is not None else "ffn",
                 x, pl.BlockSpec((tm, D), lambda t: (t, 0)), main_specs, main_args, cast_weights,
                 [pltpu.VMEM((tm, F), BF16)])


def _conv_silu(qk_ref, conv_ref, row0, col0):
    acc = None
    for j in range(ML_CONV):
        rows = pl.ds(row0 + V7X_SUBLANES - (ML_CONV - 1) + j, ML_ROWS)
        term = conv_ref[j:j + 1, col0:col0 + ML_DQK] * qk_ref[rows, col0:col0 + ML_DQK]
        acc = term if acc is None else acc + term
    return jax.nn.silu(acc)


def _mlstm_kernel(n_in, n_cast, *refs):
    ins, cast_in, o_ref, cast_out, scratch = _split_refs(refs, n_in, n_cast)
    (x_ref, win_ref, bin_ref, wgc_ref, wgr_ref, bgc_ref, bgr_ref, conv_ref, ng_ref,
     wout_ref, lng_ref, lnb_ref) = ins
    qk_ref, ct_ref, n_ref, m_ref, hb_ref = scratch
    L = ML_ROWS
    HALO = V7X_SUBLANES
    n_chunks = x_ref.shape[0] // L

    @pl.when(pl.program_id(0) == 0)
    def _init():
        qk_ref[0:HALO, :] = jnp.zeros((HALO, 2 * ML_QK), F32)
        ct_ref[...] = jnp.zeros_like(ct_ref)
        n_ref[...] = jnp.zeros_like(n_ref)
        m_ref[...] = jnp.zeros_like(m_ref)

    _run_casts(cast_in, cast_out)
    xbs = [x_ref[c * L:(c + 1) * L, :].astype(BF16) for c in range(n_chunks)]

    ri = lax.broadcasted_iota(jnp.int32, (L, L), 0)
    ci = lax.broadcasted_iota(jnp.int32, (L, L), 1)
    causal = ri >= ci
    lower = causal.astype(BF16)
    upper = (ri <= ci).astype(BF16)

    def qk_piece(c, piece):
        cols = slice(piece * WCHUNK, (piece + 1) * WCHUNK)
        qk_ref[HALO + c * L:HALO + (c + 1) * L, cols] = _dot(xbs[c], win_ref[piece]) + bin_ref[:, cols]

    def gates(c):
        g_col = _dot(xbs[c], wgc_ref[...]) + bgc_ref[...]
        g_row = _dot_nt(wgr_ref[...], xbs[c]) + bgr_ref[...]
        fh, fm, fl = _split3(_log_sigmoid(g_col))
        bcum_col = _dot(lower, fh) + _dot(lower, fm) + _dot(lower, fl)
        fh, fm, fl = _split3(_log_sigmoid(g_row))
        bcum_row = _dot(fh, upper) + _dot(fm, upper) + _dot(fl, upper)
        return g_col, g_row, bcum_col, bcum_row

    def head(c, h, gate_vals):
        g_col, g_row, bcum_col, bcum_row = gate_vals
        xb = xbs[c]
        q = _conv_silu(qk_ref, conv_ref, c * L, h * ML_DQK)
        k = _conv_silu(qk_ref, conv_ref, c * L, ML_QK + h * ML_DQK) * (ML_DQK ** -0.5)
        qb = q.astype(BF16)
        kb = k.astype(BF16)
        vlo = 2 * ML_QK + h * ML_DV
        v = _dot(xb, win_ref[vlo // WCHUNK]) + bin_ref[:, vlo:vlo + ML_DV]
        vb = v.astype(BF16)

        i_col = g_col[:, h:h + 1]
        b_col = bcum_col[:, ML_HEADS + h:ML_HEADS + h + 1]
        i_row = g_row[h:h + 1, :]
        b_row = bcum_row[ML_HEADS + h:ML_HEADS + h + 1, :]
        m_prev = m_ref[h][:, 0:1]
        ct = ct_ref[h]
        n_prev = n_ref[h]

        d_log = jnp.where(causal, b_col - b_row + i_row, -jnp.inf)
        inter = b_col + m_prev
        m_t = jnp.maximum(inter, jnp.max(d_log, axis=-1, keepdims=True))
        s_mat = _dot_nt(qb, kb) * jnp.exp(d_log - m_t)
        w_inter = jnp.exp(inter - m_t)
        num = _dot(s_mat.astype(BF16), vb) + w_inter * _dot(qb, ct.astype(BF16))
        den = jnp.sum(s_mat, axis=-1, keepdims=True) + w_inter * jnp.sum(q * n_prev, axis=-1, keepdims=True)
        hh = num / jnp.maximum(jnp.abs(den), jnp.exp(-m_t))

        b_last = b_col[L - 1:L, :]
        w_log = b_last - b_col + i_col
        m_new = jnp.maximum(b_last + m_prev, jnp.max(w_log, axis=0, keepdims=True))
        decay = jnp.exp(b_last + m_prev - m_new)
        kw = k * jnp.exp(w_log - m_new)
        ct_ref[h] = decay * ct + _dot_tn(kw.astype(BF16), vb)
        n_ref[h] = decay * n_prev + jnp.sum(kw, axis=0, keepdims=True)
        m_ref[h] = jnp.broadcast_to(m_new, m_ref.shape[1:])

        mu = jnp.mean(hh, axis=-1, keepdims=True)
        hc = hh - mu
        var = jnp.mean(hc * hc, axis=-1, keepdims=True)
        hn = hc * lax.rsqrt(var + LN_EPS) * ng_ref[:, h * ML_DV:(h + 1) * ML_DV]
        zlo = 2 * ML_QK + ML_INNER + h * ML_DV
        z = _dot(xb, win_ref[zlo // WCHUNK]) + bin_ref[:, zlo:zlo + ML_DV]
        hb_ref[c * L:(c + 1) * L, h * ML_DV:(h + 1) * ML_DV] = (hn * jax.nn.sigmoid(z)).astype(BF16)

    def out_half(c, half):
        return _dot(hb_ref[c * L:(c + 1) * L, :], wout_ref[half])

    def finish(c, halves):
        rows = slice(c * L, (c + 1) * L)
        y = jnp.concatenate(halves, axis=1)
        o_ref[rows, :] = _layer_norm(DN_ALPHA * x_ref[rows, :] + y, lng_ref[1:2, :], lnb_ref[1:2, :])

    n_pieces = 2 * ML_QK // WCHUNK
    for piece in range(n_pieces):
        qk_piece(0, piece)
    gate_vals = [gates(c) for c in range(n_chunks)]
    for c in range(n_chunks):
        halves = []
        for h in range(ML_HEADS):
            head(c, h, gate_vals[c])
            if c + 1 < n_chunks:
                qk_piece(c + 1, h)
            if c > 0 and h < 2:
                halves.append(out_half(c - 1, h))
                if h == 1:
                    finish(c - 1, halves)
    finish(n_chunks - 1, [out_half(n_chunks - 1, 0), out_half(n_chunks - 1, 1)])

    qk_ref[0:HALO, :] = qk_ref[n_chunks * L:n_chunks * L + HALO, :]


def _mlstm_call(x, w_in, b_in, wg_col, wg_row, bg_col, bg_row, conv_w, norm_g, w_out, ln_g, ln_b, layer, j,
                cast_weights):
    S, D = x.shape
    L = ML_ROWS
    main_specs = [
        _resident((ML_MAIN // WCHUNK, D, WCHUNK), (0, 0, 0)),
        _resident((None, 1, ML_MAIN), (j, 0, 0)),
        _resident((None, D, 2 * ML_HEADS), (j, 0, 0)),
        _resident((None, 4 * ML_HEADS, D), (j, 0, 0)),
        _resident((None, 1, 2 * ML_HEADS), (j, 0, 0)),
        _resident((None, 4 * ML_HEADS, 1), (j, 0, 0)),
        _resident((None, ML_CONV, 2 * ML_QK), (j, 0, 0)),
        _resident((None, 1, ML_INNER), (j, 0, 0)),
        _resident((2, ML_INNER, D // 2), (0, 0, 0)),
        _resident((None, 4, D), (layer, 0, 0)),
        _resident((None, 4, D), (layer, 0, 0)),
    ]
    main_args = [w_in, b_in, wg_col, wg_row, bg_col, bg_row, conv_w, norm_g, w_out, ln_g, ln_b]
    scratch = [
        pltpu.VMEM((V7X_SUBLANES + ML_TILE, 2 * ML_QK), F32),
        pltpu.VMEM((ML_HEADS, ML_DQK, ML_DV), F32),
        pltpu.VMEM((ML_HEADS, 1, ML_DQK), F32),
        pltpu.VMEM((ML_HEADS, 1, 128), F32),
        pltpu.VMEM((ML_TILE, ML_INNER), BF16),
    ]
    return _call(_mlstm_kernel, "mlstm", x, pl.BlockSpec((ML_TILE, D), lambda t: (t, 0)),
                 main_specs, main_args, cast_weights, scratch)


def _gmlp_kernel(n_in, n_cast, *refs):
    ins, cast_in, o_ref, cast_out, (v_ref, u_ref, y_ref) = _split_refs(refs, n_in, n_cast)
    x_ref, win_ref, bin_ref, vng_ref, vnb_ref, ws_ref, bst_ref, wout_ref, lng_ref, lnb_ref = ins
    _run_casts(cast_in, cast_out)
    xb = x_ref[...].astype(BF16)
    for g in range(GM_GROUPS):
        lo = g * GM_GDIM
        vlo = GM_WIDTH + lo
        v_ref[:, lo:lo + GM_GDIM] = _gelu(_dot(xb, win_ref[:, vlo:vlo + GM_GDIM]) + bin_ref[:, vlo:vlo + GM_GDIM])
    for g in range(GM_GROUPS):
        lo = g * GM_GDIM
        u_ref[:, lo:lo + GM_GDIM] = _gelu(_dot(xb, win_ref[:, lo:lo + GM_GDIM]) + bin_ref[:, lo:lo + GM_GDIM])
    v = v_ref[...]
    mu = jnp.mean(v, axis=-1, keepdims=True)
    vc = v - mu
    rstd = lax.rsqrt(jnp.mean(vc * vc, axis=-1, keepdims=True) + LN_EPS)

    blk_r = lax.broadcasted_iota(jnp.int32, (GM_BLOCK, GM_BLOCK), 0) // CHUNK
    blk_c = lax.broadcasted_iota(jnp.int32, (GM_BLOCK, GM_BLOCK), 1) // CHUNK
    chunk_causal = blk_r >= blk_c
    for g in range(GM_GROUPS):
        lo = g * GM_GDIM
        vn = ((v_ref[:, lo:lo + GM_GDIM] - mu) * rstd * vng_ref[:, lo:lo + GM_GDIM]
              + vnb_ref[:, lo:lo + GM_GDIM]).astype(BF16)
        wsm = jnp.where(chunk_causal, ws_ref[g], 0.0).astype(BF16)
        bias = bst_ref[:, g:g + 1]
        for nb in range(GM_ROWS // GM_BLOCK):
            r0 = nb * GM_BLOCK
            s = _dot(wsm, vn[r0:r0 + GM_BLOCK, :]) + bias
            y_ref[r0:r0 + GM_BLOCK, lo:lo + GM_GDIM] = (u_ref[r0:r0 + GM_BLOCK, lo:lo + GM_GDIM] * s).astype(BF16)
    for r in range(GM_ROWS // GM_SUB):
        rs = slice(r * GM_SUB, (r + 1) * GM_SUB)
        y = _dot(y_ref[rs, :], wout_ref[...])
        o_ref[rs, :] = _layer_norm(DN_ALPHA * x_ref[rs, :] + y, lng_ref[1:2, :], lnb_ref[1:2, :])


def _gmlp_call(x, w_in, b_in, vn_g, vn_b, ws, bs_t, w_out, ln_g, ln_b, layer, j, cast_weights):
    S, D = x.shape
    main_specs = [
        _resident((D, 2 * GM_WIDTH), (0, 0)),
        _resident((None, 1, 2 * GM_WIDTH), (j, 0, 0)),
        _resident((None, 1, GM_WIDTH), (j, 0, 0)),
        _resident((None, 1, GM_WIDTH), (j, 0, 0)),
        _resident((None, GM_GROUPS, GM_BLOCK, GM_BLOCK), (j, 0, 0, 0)),
        _resident((None, GM_BLOCK, GM_GROUPS), (j, 0, 0)),
        _resident((GM_WIDTH, D), (0, 0)),
        _resident((None, 4, D), (layer, 0, 0)),
        _resident((None, 4, D), (layer, 0, 0)),
    ]
    main_args = [w_in, b_in, vn_g, vn_b, ws, bs_t, w_out, ln_g, ln_b]
    scratch = [
        pltpu.VMEM((GM_ROWS, GM_WIDTH), F32),
        pltpu.VMEM((GM_ROWS, GM_WIDTH), F32),
        pltpu.VMEM((GM_ROWS, GM_WIDTH), BF16),
    ]
    return _call(_gmlp_kernel, "gmlp", x, pl.BlockSpec((GM_ROWS, D), lambda t: (t, 0)),
                 main_specs, main_args, cast_weights, scratch)


def kernel(x, p, ln_g, ln_b, ffn1_wgu, ffn1_wd, ffn2_wgu, ffn2_wd, ml_w_in, ml_b_in, ml_conv, ml_norm_g, ml_w_out, gm_w_in, gm_b_in, gm_vn_g, gm_vn_b, gm_ws, gm_bs, gm_w_out, ple_wp, ple_wg, ple_bg):
    B, S, D = x.shape
    assert (B, S, D) == (1, SEQ, D_MODEL)
    xs = x.reshape(S, D)
    ps = p.reshape(DEPTH, S, PLE_DIM)

    ple_bg_r = ple_bg[:, None, :]
    ml_b_main = ml_b_in[:, None, :ML_MAIN]
    ml_wg_col = ml_w_in[:, :, ML_MAIN:].astype(BF16)
    ml_wg_row = jnp.pad(jnp.swapaxes(ml_w_in[:, :, ML_MAIN:], 1, 2),
                        ((0, 0), (0, 2 * ML_HEADS), (0, 0))).astype(BF16)
    ml_bg_col = ml_b_in[:, None, ML_MAIN:]
    ml_bg_row = jnp.pad(ml_b_in[:, ML_MAIN:], ((0, 0), (0, 2 * ML_HEADS)))[:, :, None]
    ml_norm_g_r = ml_norm_g[:, None, :]
    ml_w_in_t = jnp.swapaxes(ml_w_in, 1, 2)
    gm_b_in_r = gm_b_in[:, None, :]
    gm_vn_g_r, gm_vn_b_r = gm_vn_g[:, None, :], gm_vn_b[:, None, :]
    gm_bs_t = jnp.swapaxes(gm_bs, 1, 2)

    ffn1_w = (ffn1_wgu[0].astype(BF16), ffn1_wd[0].astype(BF16))
    for i in range(DEPTH):
        j = i // 2
        if i % 2 == 0:
            mixer_f32 = [("transposed", ml_w_in_t, j), ("halves", ml_w_out, j)]
        else:
            mixer_f32 = [("plain", gm_w_in, j), ("plain", gm_w_out, j)]
        ffn2_rest_f32 = [("plain", ffn2_wd, i), ("plain", ple_wg, i), ("plain", ple_wp, i)]
        xs, cast_w = _ffn_call(xs, *ffn1_w, ln_g, ln_b, i, 0, mixer_f32 + ffn2_rest_f32)
        mixer_w, ffn2_rest_w = cast_w[:2], cast_w[2:]
        ffn2_f32 = [("plain", ffn2_wgu, i)]
        if i % 2 == 0:
            xs, ffn2_w = _mlstm_call(xs, mixer_w[0], ml_b_main, ml_wg_col, ml_wg_row, ml_bg_col, ml_bg_row,
                                     ml_conv, ml_norm_g_r, mixer_w[1], ln_g, ln_b, i, j, ffn2_f32)
        else:
            xs, ffn2_w = _gmlp_call(xs, mixer_w[0], gm_b_in_r, gm_vn_g_r, gm_vn_b_r, gm_ws, gm_bs_t,
                                    mixer_w[1], ln_g, ln_b, i, j, ffn2_f32)
        next_f32 = [("plain", ffn1_wgu, i + 1), ("plain", ffn1_wd, i + 1)] if i + 1 < DEPTH else []
        xs, ffn1_w = _ffn_call(xs, ffn2_w[0], ffn2_rest_w[0], ln_g, ln_b, i, 2, next_f32,
                               ple=(ps, ffn2_rest_w[1], ple_bg_r, ffn2_rest_w[2]))
    return xs.reshape(B, S, D)
```

```python
import functools

import jax
import jax.numpy as jnp
from jax import lax
from jax.experimental import pallas as pl
from jax.experimental.pallas import tpu as pltpu

D_MODEL = 1024
SEQ = 16384
DEPTH = 4
CHUNK = 64
PLE_DIM = 256
FFN_HIDDEN = 2816
ML_HEADS = 4
ML_DQK = 256
ML_DV = 512
ML_INNER = ML_HEADS * ML_DV
ML_QK = ML_HEADS * ML_DQK
ML_MAIN = 2 * ML_QK + 2 * ML_INNER
ML_CONV = 4
GM_BLOCK = 128
GM_GROUPS = 8
GM_WIDTH = 2 * D_MODEL
GM_GDIM = GM_WIDTH // GM_GROUPS
DN_ALPHA = (2.0 * DEPTH) ** 0.25
LN_EPS = 1e-5

V7X_MXU_DIM = 256
V7X_SUBLANES = 8
V7X_LANES = 128
V7X_BF16_ROWS = 16
V7X_VMEM_BYTES = 64 * 1024 * 1024
VMEM_LIMIT_BYTES = V7X_VMEM_BYTES * 7 // 8

FFN_ROWS = 1024
FFN_PLE_ROWS = 1024
FFN_COLS = V7X_MXU_DIM
FFN_SUB = 256
ML_ROWS = 256
ML_TILE = 512
GM_ROWS = 1024
GM_SUB = 256
WCHUNK = 512

F32 = jnp.float32
BF16 = jnp.bfloat16


def _dot(a, b):
    return jnp.dot(a, b, preferred_element_type=F32)


def _dot_nt(a, b):
    return lax.dot_general(a, b, (((1,), (1,)), ((), ())), preferred_element_type=F32)


def _dot_tn(a, b):
    return lax.dot_general(a, b, (((0,), (0,)), ((), ())), preferred_element_type=F32)


def _layer_norm(z, g, b):
    mu = jnp.mean(z, axis=-1, keepdims=True)
    zc = z - mu
    var = jnp.mean(zc * zc, axis=-1, keepdims=True)
    return zc * lax.rsqrt(var + LN_EPS) * g + b


def _split3(a):
    hi = a.astype(BF16)
    r1 = a - hi.astype(F32)
    mid = r1.astype(BF16)
    lo = (r1 - mid.astype(F32)).astype(BF16)
    return hi, mid, lo


def _gelu(x):
    return 0.5 * x * (1.0 + lax.erf(x * (2.0 ** -0.5)))


def _log_sigmoid(x):
    return jnp.minimum(x, 0.0) - jnp.log1p(jnp.exp(-jnp.abs(x)))


def _resident(shape, index):
    return pl.BlockSpec(shape, lambda t: index, pipeline_mode=pl.Buffered(1))


def _cast_plan(weights, steps):
    in_specs, out_specs, out_shapes, args = [], [], [], []
    for kind, w, layer in weights:
        args.append(w)
        if kind == "transposed":
            _, N, K = w.shape
            last = N // WCHUNK - 1
            assert steps > last
            in_specs.append(pl.BlockSpec((None, WCHUNK, K),
                                         lambda t, layer=layer, last=last: (layer, jnp.minimum(t, last), 0)))
            out_specs.append(pl.BlockSpec((None, K, WCHUNK), lambda t, last=last: (jnp.minimum(t, last), 0, 0)))
            out_shapes.append(jax.ShapeDtypeStruct((last + 1, K, WCHUNK), BF16))
            continue
        _, K, N = w.shape
        hold = 1
        while K % (steps // hold) or (K // (steps // hold)) % V7X_BF16_ROWS:
            hold *= 2
        rows = K // (steps // hold)
        in_specs.append(pl.BlockSpec((None, rows, N), lambda t, layer=layer, hold=hold: (layer, t // hold, 0)))
        if kind == "halves":
            out_specs.append(pl.BlockSpec((2, rows, N // 2), lambda t, hold=hold: (0, t // hold, 0)))
            out_shapes.append(jax.ShapeDtypeStruct((2, K, N // 2), BF16))
        else:
            out_specs.append(pl.BlockSpec((rows, N), lambda t, hold=hold: (t // hold, 0)))
            out_shapes.append(jax.ShapeDtypeStruct((K, N), BF16))
    return in_specs, out_specs, out_shapes, args


def _split_refs(refs, n_in, n_cast):
    a, b, c = n_in, n_in + n_cast, n_in + 2 * n_cast + 1
    return refs[:a], refs[a:b], refs[b], refs[b + 1:c], refs[c:]


def _run_casts(cast_in, cast_out):
    for src, dst in zip(cast_in, cast_out):
        if len(dst.shape) == 3:
            half = dst.shape[2]
            dst[0] = src[:, :half].astype(BF16)
            dst[1] = src[:, half:].astype(BF16)
        elif dst.shape != src.shape:
            dst[...] = src[...].T.astype(BF16)
        else:
            dst[...] = src[...].astype(BF16)


def _call(kernel_fn, name, x, row_spec, main_specs, main_args, cast_weights, scratch_shapes):
    S, D = x.shape
    steps = S // row_spec.block_shape[0]
    c_in, c_out, c_shapes, c_args = _cast_plan(cast_weights, steps)
    outs = pl.pallas_call(
        functools.partial(kernel_fn, 1 + len(main_args), len(c_args)),
        grid=(steps,),
        in_specs=[row_spec] + main_specs + c_in,
        out_specs=[row_spec] + c_out,
        out_shape=[jax.ShapeDtypeStruct((S, D), F32)] + c_shapes,
        scratch_shapes=scratch_shapes,
        compiler_params=pltpu.CompilerParams(
            dimension_semantics=("arbitrary",), vmem_limit_bytes=VMEM_LIMIT_BYTES),
        name=name,
    )(x, *main_args, *c_args)
    return outs[0], outs[1:]


def _ffn_kernel(with_ple, ln_row, n_in, n_cast, *refs):
    ins, cast_in, o_ref, cast_out, (a_ref,) = _split_refs(refs, n_in, n_cast)
    if with_ple:
        x_ref, wgu_ref, wd_ref, lng_ref, lnb_ref, p_ref, wpg_ref, bpg_ref, wpp_ref = ins
    else:
        x_ref, wgu_ref, wd_ref, lng_ref, lnb_ref = ins
    _run_casts(cast_in, cast_out)
    xb = x_ref[...].astype(BF16)
    for c in range(FFN_HIDDEN // FFN_COLS):
        lo = c * FFN_COLS
        g = _dot(xb, wgu_ref[:, lo:lo + FFN_COLS])
        u = _dot(xb, wgu_ref[:, FFN_HIDDEN + lo:FFN_HIDDEN + lo + FFN_COLS])
        a_ref[:, lo:lo + FFN_COLS] = (jax.nn.silu(g) * u).astype(BF16)
    n_sub = a_ref.shape[0] // FFN_SUB

    def down(r):
        rs = slice(r * FFN_SUB, (r + 1) * FFN_SUB)
        y = _dot(a_ref[rs, :], wd_ref[...])
        return _layer_norm(DN_ALPHA * x_ref[rs, :] + 0.5 * y,
                           lng_ref[ln_row:ln_row + 1, :], lnb_ref[ln_row:ln_row + 1, :])

    x1 = down(0)
    for r in range(n_sub):
        rs = slice(r * FFN_SUB, (r + 1) * FFN_SUB)
        x1_next = down(r + 1) if r + 1 < n_sub else None
        if with_ple:
            gate = jax.nn.sigmoid(_dot(x1.astype(BF16), wpg_ref[...]) + bpg_ref[...])
            emb = _dot(p_ref[rs, :].astype(BF16), wpp_ref[...])
            x1 = _layer_norm(DN_ALPHA * x1 + gate * emb, lng_ref[3:4, :], lnb_ref[3:4, :])
        o_ref[rs, :] = x1
        x1 = x1_next


def _ffn_call(x, wgu, wd, ln_g, ln_b, layer, ln_row, cast_weights, ple=None):
    S, D = x.shape
    F = FFN_HIDDEN
    tm = FFN_ROWS if ple is None else FFN_PLE_ROWS
    main_specs = [
        _resident((D, 2 * F), (0, 0)),
        _resident((F, D), (0, 0)),
        _resident((None, 4, D), (layer, 0, 0)),
        _resident((None, 4, D), (layer, 0, 0)),
    ]
    main_args = [wgu, wd, ln_g, ln_b]
    if ple is not None:
        p, wpg, bpg, wpp = ple
        main_specs += [
            pl.BlockSpec((None, tm, PLE_DIM), lambda t: (layer, t, 0)),
            _resident((D, D), (0, 0)),
            _resident((None, 1, D), (layer, 0, 0)),
            _resident((PLE_DIM, D), (0, 0)),
        ]
        main_args += [p, wpg, bpg, wpp]
    return _call(functools.partial(_ffn_kernel, ple is not None, ln_row),
                 "ffn_ple" if ple is not None else "ffn",
                 x, pl.BlockSpec((tm, D), lambda t: (t, 0)), main_specs, main_args, cast_weights,
                 [pltpu.VMEM((tm, F), BF16)])


def _conv_silu(qk_ref, act_ref, conv_ref, row0, col0, scale):
    L = ML_ROWS
    HALO = V7X_SUBLANES
    groups = L // V7X_SUBLANES
    outs = []
    for b in range(ML_DQK // V7X_LANES):
        blk = col0 // V7X_LANES + b
        acc = None
        for j in range(ML_CONV):
            start = row0 + HALO - (ML_CONV - 1) + j
            tap = jnp.concatenate(
                [qk_ref[blk, pl.ds(start + s, groups, stride=V7X_SUBLANES), :] for s in range(V7X_SUBLANES)], axis=0)
            term = conv_ref[j:j + 1, blk * V7X_LANES:(blk + 1) * V7X_LANES] * tap
            acc = term if acc is None else acc + term
        act = jax.nn.silu(acc) * scale if scale != 1.0 else jax.nn.silu(acc)
        for s in range(V7X_SUBLANES):
            act_ref[blk, pl.ds(row0 + s, groups, stride=V7X_SUBLANES), :] = act[s * groups:(s + 1) * groups, :]
        outs.append(act_ref[blk, row0:row0 + L, :])
    return jnp.concatenate(outs, axis=1)


def _mlstm_kernel(n_in, n_cast, *refs):
    ins, cast_in, o_ref, cast_out, scratch = _split_refs(refs, n_in, n_cast)
    (x_ref, win_ref, bin_ref, wgc_ref, wgr_ref, bgc_ref, bgr_ref, conv_ref, ng_ref,
     wout_ref, lng_ref, lnb_ref) = ins
    qk_ref, act_ref, ct_ref, n_ref, m_ref, hb_ref = scratch
    L = ML_ROWS
    HALO = V7X_SUBLANES
    n_chunks = x_ref.shape[0] // L
    n_blk = WCHUNK // V7X_LANES

    @pl.when(pl.program_id(0) == 0)
    def _init():
        qk_ref[:, 0:HALO, :] = jnp.zeros((2 * ML_QK // V7X_LANES, HALO, V7X_LANES), F32)
        ct_ref[...] = jnp.zeros_like(ct_ref)
        n_ref[...] = jnp.zeros_like(n_ref)
        m_ref[...] = jnp.zeros_like(m_ref)

    _run_casts(cast_in, cast_out)
    xbs = [x_ref[c * L:(c + 1) * L, :].astype(BF16) for c in range(n_chunks)]

    ri = lax.broadcasted_iota(jnp.int32, (L, L), 0)
    ci = lax.broadcasted_iota(jnp.int32, (L, L), 1)
    causal = ri >= ci
    lower = causal.astype(BF16)
    upper = (ri <= ci).astype(BF16)

    def qk_piece(c, piece):
        pre = _dot(xbs[c], win_ref[piece]) + bin_ref[:, piece * WCHUNK:(piece + 1) * WCHUNK]
        for b in range(n_blk):
            qk_ref[piece * n_blk + b, HALO + c * L:HALO + (c + 1) * L, :] = pre[:, b * V7X_LANES:(b + 1) * V7X_LANES]

    def gates(c):
        g_col = _dot(xbs[c], wgc_ref[...]) + bgc_ref[...]
        g_row = _dot_nt(wgr_ref[...], xbs[c]) + bgr_ref[...]
        fh, fm, fl = _split3(_log_sigmoid(g_col))
        bcum_col = _dot(lower, fh) + _dot(lower, fm) + _dot(lower, fl)
        fh, fm, fl = _split3(_log_sigmoid(g_row))
        bcum_row = _dot(fh, upper) + _dot(fm, upper) + _dot(fl, upper)
        return g_col, g_row, bcum_col, bcum_row

    def head(c, h, gate_vals):
        g_col, g_row, bcum_col, bcum_row = gate_vals
        xb = xbs[c]
        q = _conv_silu(qk_ref, act_ref, conv_ref, c * L, h * ML_DQK, 1.0)
        k = _conv_silu(qk_ref, act_ref, conv_ref, c * L, ML_QK + h * ML_DQK, ML_DQK ** -0.5)
        qb = q.astype(BF16)
        kb = k.astype(BF16)
        vlo = 2 * ML_QK + h * ML_DV
        v = _dot(xb, win_ref[vlo // WCHUNK]) + bin_ref[:, vlo:vlo + ML_DV]
        vb = v.astype(BF16)

        i_col = g_col[:, h:h + 1]
        b_col = bcum_col[:, ML_HEADS + h:ML_HEADS + h + 1]
        i_row = g_row[h:h + 1, :]
        b_row = bcum_row[ML_HEADS + h:ML_HEADS + h + 1, :]
        m_prev = m_ref[h][:, 0:1]
        ct = ct_ref[h]
        n_prev = n_ref[h]

        d_log = jnp.where(causal, b_col - b_row + i_row, -jnp.inf)
        inter = b_col + m_prev
        m_t = jnp.maximum(inter, jnp.max(d_log, axis=-1, keepdims=True))
        s_mat = _dot_nt(qb, kb) * jnp.exp(d_log - m_t)
        w_inter = jnp.exp(inter - m_t)
        num = _dot(s_mat.astype(BF16), vb) + w_inter * _dot(qb, ct.astype(BF16))
        den = jnp.sum(s_mat, axis=-1, keepdims=True) + w_inter * jnp.sum(q * n_prev, axis=-1, keepdims=True)
        hh = num / jnp.maximum(jnp.abs(den), jnp.exp(-m_t))

        b_last = b_col[L - 1:L, :]
        w_log = b_last - b_col + i_col
        m_new = jnp.maximum(b_last + m_prev, jnp.max(w_log, axis=0, keepdims=True))
        decay = jnp.exp(b_last + m_prev - m_new)
        kw = k * jnp.exp(w_log - m_new)
        ct_ref[h] = decay * ct + _dot_tn(kw.astype(BF16), vb)
        n_ref[h] = decay * n_prev + jnp.sum(kw, axis=0, keepdims=True)
        m_ref[h] = jnp.broadcast_to(m_new, m_ref.shape[1:])

        mu = jnp.mean(hh, axis=-1, keepdims=True)
        hc = hh - mu
        var = jnp.mean(hc * hc, axis=-1, keepdims=True)
        hn = hc * lax.rsqrt(var + LN_EPS) * ng_ref[:, h * ML_DV:(h + 1) * ML_DV]
        zlo = 2 * ML_QK + ML_INNER + h * ML_DV
        z = _dot(xb, win_ref[zlo // WCHUNK]) + bin_ref[:, zlo:zlo + ML_DV]
        hb_ref[c * L:(c + 1) * L, h * ML_DV:(h + 1) * ML_DV] = (hn * jax.nn.sigmoid(z)).astype(BF16)

    def out_half(c, half):
        return _dot(hb_ref[c * L:(c + 1) * L, :], wout_ref[half])

    def finish(c, halves):
        rows = slice(c * L, (c + 1) * L)
        y = jnp.concatenate(halves, axis=1)
        o_ref[rows, :] = _layer_norm(DN_ALPHA * x_ref[rows, :] + y, lng_ref[1:2, :], lnb_ref[1:2, :])

    n_pieces = 2 * ML_QK // WCHUNK
    for piece in range(n_pieces):
        qk_piece(0, piece)
    gate_vals = [gates(c) for c in range(n_chunks)]
    for c in range(n_chunks):
        halves = []
        for h in range(ML_HEADS):
            head(c, h, gate_vals[c])
            if c + 1 < n_chunks:
                qk_piece(c + 1, h)
            if c > 0 and h < 2:
                halves.append(out_half(c - 1, h))
                if h == 1:
                    finish(c - 1, halves)
    finish(n_chunks - 1, [out_half(n_chunks - 1, 0), out_half(n_chunks - 1, 1)])

    qk_ref[:, 0:HALO, :] = qk_ref[:, n_chunks * L:n_chunks * L + HALO, :]


def _mlstm_call(x, w_in, b_in, wg_col, wg_row, bg_col, bg_row, conv_w, norm_g, w_out, ln_g, ln_b, layer, j,
                cast_weights):
    S, D = x.shape
    L = ML_ROWS
    main_specs = [
        _resident((ML_MAIN // WCHUNK, D, WCHUNK), (0, 0, 0)),
        _resident((None, 1, ML_MAIN), (j, 0, 0)),
        _resident((None, D, 2 * ML_HEADS), (j, 0, 0)),
        _resident((None, 4 * ML_HEADS, D), (j, 0, 0)),
        _resident((None, 1, 2 * ML_HEADS), (j, 0, 0)),
        _resident((None, 4 * ML_HEADS, 1), (j, 0, 0)),
        _resident((None, ML_CONV, 2 * ML_QK), (j, 0, 0)),
        _resident((None, 1, ML_INNER), (j, 0, 0)),
        _resident((2, ML_INNER, D // 2), (0, 0, 0)),
        _resident((None, 4, D), (layer, 0, 0)),
        _resident((None, 4, D), (layer, 0, 0)),
    ]
    main_args = [w_in, b_in, wg_col, wg_row, bg_col, bg_row, conv_w, norm_g, w_out, ln_g, ln_b]
    scratch = [
        pltpu.VMEM((2 * ML_QK // V7X_LANES, V7X_SUBLANES + ML_TILE, V7X_LANES), F32),
        pltpu.VMEM((2 * ML_QK // V7X_LANES, ML_TILE, V7X_LANES), F32),
        pltpu.VMEM((ML_HEADS, ML_DQK, ML_DV), F32),
        pltpu.VMEM((ML_HEADS, 1, ML_DQK), F32),
        pltpu.VMEM((ML_HEADS, 1, 128), F32),
        pltpu.VMEM((ML_TILE, ML_INNER), BF16),
    ]
    return _call(_mlstm_kernel, "mlstm", x, pl.BlockSpec((ML_TILE, D), lambda t: (t, 0)),
                 main_specs, main_args, cast_weights, scratch)


def _gmlp_kernel(n_in, n_cast, *refs):
    ins, cast_in, o_ref, cast_out, (v_ref, u_ref, y_ref) = _split_refs(refs, n_in, n_cast)
    x_ref, win_ref, bin_ref, vng_ref, vnb_ref, ws_ref, bst_ref, wout_ref, lng_ref, lnb_ref = ins
    _run_casts(cast_in, cast_out)
    xb = x_ref[...].astype(BF16)
    for g in range(GM_GROUPS):
        lo = g * GM_GDIM
        vlo = GM_WIDTH + lo
        v_ref[:, lo:lo + GM_GDIM] = _gelu(_dot(xb, win_ref[:, vlo:vlo + GM_GDIM]) + bin_ref[:, vlo:vlo + GM_GDIM])
    for g in range(GM_GROUPS):
        lo = g * GM_GDIM
        u_ref[:, lo:lo + GM_GDIM] = _gelu(_dot(xb, win_ref[:, lo:lo + GM_GDIM]) + bin_ref[:, lo:lo + GM_GDIM])
    v = v_ref[...]
    mu = jnp.mean(v, axis=-1, keepdims=True)
    vc = v - mu
    rstd = lax.rsqrt(jnp.mean(vc * vc, axis=-1, keepdims=True) + LN_EPS)

    blk_r = lax.broadcasted_iota(jnp.int32, (GM_BLOCK, GM_BLOCK), 0) // CHUNK
    blk_c = lax.broadcasted_iota(jnp.int32, (GM_BLOCK, GM_BLOCK), 1) // CHUNK
    chunk_causal = blk_r >= blk_c
    for g in range(GM_GROUPS):
        lo = g * GM_GDIM
        vn = ((v_ref[:, lo:lo + GM_GDIM] - mu) * rstd * vng_ref[:, lo:lo + GM_GDIM]
              + vnb_ref[:, lo:lo + GM_GDIM]).astype(BF16)
        wsm = jnp.where(chunk_causal, ws_ref[g], 0.0).astype(BF16)
        bias = bst_ref[:, g:g + 1]
        for nb in range(GM_ROWS // GM_BLOCK):
            r0 = nb * GM_BLOCK
            s = _dot(wsm, vn[r0:r0 + GM_BLOCK, :]) + bias
            y_ref[r0:r0 + GM_BLOCK, lo:lo + GM_GDIM] = (u_ref[r0:r0 + GM_BLOCK, lo:lo + GM_GDIM] * s).astype(BF16)
    for r in range(GM_ROWS // GM_SUB):
        rs = slice(r * GM_SUB, (r + 1) * GM_SUB)
        y = _dot(y_ref[rs, :], wout_ref[...])
        o_ref[rs, :] = _layer_norm(DN_ALPHA * x_ref[rs, :] + y, lng_ref[1:2, :], lnb_ref[1:2, :])


def _gmlp_call(x, w_in, b_in, vn_g, vn_b, ws, bs_t, w_out, ln_g, ln_b, layer, j, cast_weights):
    S, D = x.shape
    main_specs = [
        _resident((D, 2 * GM_WIDTH), (0, 0)),
        _resident((None, 1, 2 * GM_WIDTH), (j, 0, 0)),
        _resident((None, 1, GM_WIDTH), (j, 0, 0)),
        _resident((None, 1, GM_WIDTH), (j, 0, 0)),
        _resident((None, GM_GROUPS, GM_BLOCK, GM_BLOCK), (j, 0, 0, 0)),
        _resident((None, GM_BLOCK, GM_GROUPS), (j, 0, 0)),
        _resident((GM_WIDTH, D), (0, 0)),
        _resident((None, 4, D), (layer, 0, 0)),
        _resident((None, 4, D), (layer, 0, 0)),
    ]
    main_args = [w_in, b_in, vn_g, vn_b, ws, bs_t, w_out, ln_g, ln_b]
    scratch = [
        pltpu.VMEM((GM_ROWS, GM_WIDTH), F32),
        pltpu.VMEM((GM_ROWS, GM_WIDTH), F32),
        pltpu.VMEM((GM_ROWS, GM_WIDTH), BF16),
    ]
    return _call(_gmlp_kernel, "gmlp", x, pl.BlockSpec((GM_ROWS, D), lambda t: (t, 0)),
                 main_specs, main_args, cast_weights, scratch)


def kernel(x, p, ln_g, ln_b, ffn1_wgu, ffn1_wd, ffn2_wgu, ffn2_wd, ml_w_in, ml_b_in, ml_conv, ml_norm_g, ml_w_out, gm_w_in, gm_b_in, gm_vn_g, gm_vn_b, gm_ws, gm_bs, gm_w_out, ple_wp, ple_wg, ple_bg):
    B, S, D = x.shape
    assert (B, S, D) == (1, SEQ, D_MODEL)
    xs = x.reshape(S, D)
    ps = p.reshape(DEPTH, S, PLE_DIM)

    ple_bg_r = ple_bg[:, None, :]
    ml_b_main = ml_b_in[:, None, :ML_MAIN]
    ml_wg_col = ml_w_in[:, :, ML_MAIN:].astype(BF16)
    ml_wg_row = jnp.pad(jnp.swapaxes(ml_w_in[:, :, ML_MAIN:], 1, 2),
                        ((0, 0), (0, 2 * ML_HEADS), (0, 0))).astype(BF16)
    ml_bg_col = ml_b_in[:, None, ML_MAIN:]
    ml_bg_row = jnp.pad(ml_b_in[:, ML_MAIN:], ((0, 0), (0, 2 * ML_HEADS)))[:, :, None]
    ml_norm_g_r = ml_norm_g[:, None, :]
    ml_w_in_t = jnp.swapaxes(ml_w_in, 1, 2)
    gm_b_in_r = gm_b_in[:, None, :]
    gm_vn_g_r, gm_vn_b_r = gm_vn_g[:, None, :], gm_vn_b[:, None, :]
    gm_bs_t = jnp.swapaxes(gm_bs, 1, 2)

    ffn1_w = (ffn1_wgu[0].astype(BF16), ffn1_wd[0].astype(BF16))
    for i in range(DEPTH):
        j = i // 2
        if i % 2 == 0:
            mixer_f32 = [("transposed", ml_w_in_t, j), ("halves", ml_w_out, j)]
        else:
            mixer_f32 = [("plain", gm_w_in, j), ("plain", gm_w_out, j)]
        ffn2_rest_f32 = [("plain", ffn2_wd, i), ("plain", ple_wg, i), ("plain", ple_wp, i)]
        xs, cast_w = _ffn_call(xs, *ffn1_w, ln_g, ln_b, i, 0, mixer_f32 + ffn2_rest_f32)
        mixer_w, ffn2_rest_w = cast_w[:2], cast_w[2:]
        ffn2_f32 = [("plain", ffn2_wgu, i)]
        if i % 2 == 0:
            xs, ffn2_w = _mlstm_call(xs, mixer_w[0], ml_b_main, ml_wg_col, ml_wg_row, ml_bg_col, ml_bg_row,
                                     ml_conv, ml_norm_g_r, mixer_w[1], ln_g, ln_b, i, j, ffn2_f32)
        else:
            xs, ffn2_w = _gmlp_call(xs, mixer_w[0], gm_b_in_r, gm_vn_g_r, gm_vn_b_r, gm_ws, gm_bs_t,
                                    mixer_w[1], ln_g, ln_b, i, j, ffn2_f32)
        next_f32 = [("plain", ffn1_wgu, i + 1), ("plain", ffn1_wd, i + 1)] if i + 1 < DEPTH else []
        xs, ffn1_w = _ffn_call(xs, ffn2_w[0], ffn2_rest_w[0], ln_g, ln_b, i, 2, next_f32,
                               ple=(ps, ffn2_rest_w[1], ple_bg_r, ffn2_rest_w[2]))
    return xs.reshape(B, S, D)
```

```python
import functools

import jax
import jax.numpy as jnp
from jax import lax
from jax.experimental import pallas as pl
from jax.experimental.pallas import tpu as pltpu

D_MODEL = 1024
SEQ = 16384
DEPTH = 4
CHUNK = 64
PLE_DIM = 256
FFN_HIDDEN = 2816
ML_HEADS = 4
ML_DQK = 256
ML_DV = 512
ML_INNER = ML_HEADS * ML_DV
ML_QK = ML_HEADS * ML_DQK
ML_MAIN = 2 * ML_QK + 2 * ML_INNER
ML_CONV = 4
GM_BLOCK = 128
GM_GROUPS = 8
GM_WIDTH = 2 * D_MODEL
GM_GDIM = GM_WIDTH // GM_GROUPS
DN_ALPHA = (2.0 * DEPTH) ** 0.25
LN_EPS = 1e-5

V7X_MXU_DIM = 256
V7X_SUBLANES = 8
V7X_LANES = 128
V7X_BF16_ROWS = 16
V7X_VMEM_BYTES = 64 * 1024 * 1024
VMEM_LIMIT_BYTES = V7X_VMEM_BYTES * 7 // 8

FFN_ROWS = 1024
FFN_PLE_ROWS = 1024
FFN_COLS = V7X_MXU_DIM
FFN_SUB = 256
ML_ROWS = 256
ML_TILE = 512
GM_ROWS = 1024
GM_SUB = 256
WCHUNK = 512

F32 = jnp.float32
BF16 = jnp.bfloat16


def _dot(a, b):
    return jnp.dot(a, b, preferred_element_type=F32)


def _dot_nt(a, b):
    return lax.dot_general(a, b, (((1,), (1,)), ((), ())), preferred_element_type=F32)


def _dot_tn(a, b):
    return lax.dot_general(a, b, (((0,), (0,)), ((), ())), preferred_element_type=F32)


def _layer_norm(z, g, b):
    mu = jnp.mean(z, axis=-1, keepdims=True)
    zc = z - mu
    var = jnp.mean(zc * zc, axis=-1, keepdims=True)
    return zc * lax.rsqrt(var + LN_EPS) * g + b


def _split3(a):
    hi = a.astype(BF16)
    r1 = a - hi.astype(F32)
    mid = r1.astype(BF16)
    lo = (r1 - mid.astype(F32)).astype(BF16)
    return hi, mid, lo


def _gelu(x):
    return 0.5 * x * (1.0 + lax.erf(x * (2.0 ** -0.5)))


def _log_sigmoid(x):
    return jnp.minimum(x, 0.0) - jnp.log1p(jnp.exp(-jnp.abs(x)))


def _resident(shape, index):
    return pl.BlockSpec(shape, lambda t: index, pipeline_mode=pl.Buffered(1))


def _cast_plan(weights, steps):
    in_specs, out_specs, out_shapes, args = [], [], [], []
    for kind, w, layer in weights:
        args.append(w)
        if kind == "transposed":
            _, N, K = w.shape
            last = N // WCHUNK - 1
            assert steps > last
            in_specs.append(pl.BlockSpec((None, WCHUNK, K),
                                         lambda t, layer=layer, last=last: (layer, jnp.minimum(t, last), 0)))
            out_specs.append(pl.BlockSpec((None, K, WCHUNK), lambda t, last=last: (jnp.minimum(t, last), 0, 0)))
            out_shapes.append(jax.ShapeDtypeStruct((last + 1, K, WCHUNK), BF16))
            continue
        _, K, N = w.shape
        hold = 1
        while K % (steps // hold) or (K // (steps // hold)) % V7X_BF16_ROWS:
            hold *= 2
        rows = K // (steps // hold)
        in_specs.append(pl.BlockSpec((None, rows, N), lambda t, layer=layer, hold=hold: (layer, t // hold, 0)))
        if kind == "halves":
            out_specs.append(pl.BlockSpec((2, rows, N // 2), lambda t, hold=hold: (0, t // hold, 0)))
            out_shapes.append(jax.ShapeDtypeStruct((2, K, N // 2), BF16))
        else:
            out_specs.append(pl.BlockSpec((rows, N), lambda t, hold=hold: (t // hold, 0)))
            out_shapes.append(jax.ShapeDtypeStruct((K, N), BF16))
    return in_specs, out_specs, out_shapes, args


def _split_refs(refs, n_in, n_cast):
    a, b, c = n_in, n_in + n_cast, n_in + 2 * n_cast + 1
    return refs[:a], refs[a:b], refs[b], refs[b + 1:c], refs[c:]


def _run_casts(cast_in, cast_out):
    for src, dst in zip(cast_in, cast_out):
        if len(dst.shape) == 3:
            half = dst.shape[2]
            dst[0] = src[:, :half].astype(BF16)
            dst[1] = src[:, half:].astype(BF16)
        elif dst.shape != src.shape:
            dst[...] = src[...].T.astype(BF16)
        else:
            dst[...] = src[...].astype(BF16)


def _call(kernel_fn, name, x, row_spec, main_specs, main_args, cast_weights, scratch_shapes):
    S, D = x.shape
    steps = S // row_spec.block_shape[0]
    c_in, c_out, c_shapes, c_args = _cast_plan(cast_weights, steps)
    outs = pl.pallas_call(
        functools.partial(kernel_fn, 1 + len(main_args), len(c_args)),
        grid=(steps,),
        in_specs=[row_spec] + main_specs + c_in,
        out_specs=[row_spec] + c_out,
        out_shape=[jax.ShapeDtypeStruct((S, D), F32)] + c_shapes,
        scratch_shapes=scratch_shapes,
        compiler_params=pltpu.CompilerParams(
            dimension_semantics=("arbitrary",), vmem_limit_bytes=VMEM_LIMIT_BYTES),
        name=name,
    )(x, *main_args, *c_args)
    return outs[0], outs[1:]


def _ffn_kernel(with_ple, ln_row, n_in, n_cast, *refs):
    ins, cast_in, o_ref, cast_out, (a_ref,) = _split_refs(refs, n_in, n_cast)
    if with_ple:
        x_ref, wgu_ref, wd_ref, lng_ref, lnb_ref, p_ref, wpg_ref, bpg_ref, wpp_ref = ins
    else:
        x_ref, wgu_ref, wd_ref, lng_ref, lnb_ref = ins
    xb = x_ref[...].astype(BF16)
    for c in range(FFN_HIDDEN // FFN_COLS):
        lo = c * FFN_COLS
        g = _dot(xb, wgu_ref[:, lo:lo + FFN_COLS])
        u = _dot(xb, wgu_ref[:, FFN_HIDDEN + lo:FFN_HIDDEN + lo + FFN_COLS])
        a_ref[:, lo:lo + FFN_COLS] = (jax.nn.silu(g) * u).astype(BF16)
        if c == 0:
            _run_casts(cast_in, cast_out)
    n_sub = a_ref.shape[0] // FFN_SUB

    def down(r):
        rs = slice(r * FFN_SUB, (r + 1) * FFN_SUB)
        y = _dot(a_ref[rs, :], wd_ref[...])
        return _layer_norm(DN_ALPHA * x_ref[rs, :] + 0.5 * y,
                           lng_ref[ln_row:ln_row + 1, :], lnb_ref[ln_row:ln_row + 1, :])

    x1 = down(0)
    for r in range(n_sub):
        rs = slice(r * FFN_SUB, (r + 1) * FFN_SUB)
        x1_next = down(r + 1) if r + 1 < n_sub else None
        if with_ple:
            gate = jax.nn.sigmoid(_dot(x1.astype(BF16), wpg_ref[...]) + bpg_ref[...])
            emb = _dot(p_ref[rs, :].astype(BF16), wpp_ref[...])
            x1 = _layer_norm(DN_ALPHA * x1 + gate * emb, lng_ref[3:4, :], lnb_ref[3:4, :])
        o_ref[rs, :] = x1
        x1 = x1_next


def _ffn_call(x, wgu, wd, ln_g, ln_b, layer, ln_row, cast_weights, ple=None):
    S, D = x.shape
    F = FFN_HIDDEN
    tm = FFN_ROWS if ple is None else FFN_PLE_ROWS
    main_specs = [
        _resident((D, 2 * F), (0, 0)),
        _resident((F, D), (0, 0)),
        _resident((None, 4, D), (layer, 0, 0)),
        _resident((None, 4, D), (layer, 0, 0)),
    ]
    main_args = [wgu, wd, ln_g, ln_b]
    if ple is not None:
        p, wpg, bpg, wpp = ple
        main_specs += [
            pl.BlockSpec((None, tm, PLE_DIM), lambda t: (layer, t, 0)),
            _resident((D, D), (0, 0)),
            _resident((None, 1, D), (layer, 0, 0)),
            _resident((PLE_DIM, D), (0, 0)),
        ]
        main_args += [p, wpg, bpg, wpp]
    return _call(functools.partial(_ffn_kernel, ple is not None, ln_row),
                 "ffn_ple" if ple is not None else "ffn",
                 x, pl.BlockSpec((tm, D), lambda t: (t, 0)), main_specs, main_args, cast_weights,
                 [pltpu.VMEM((tm, F), BF16)])


def _conv_silu(qk_ref, act_ref, conv_ref, row0, col0, scale):
    L = ML_ROWS
    HALO = V7X_SUBLANES
    groups = L // V7X_SUBLANES
    outs = []
    for b in range(ML_DQK // V7X_LANES):
        blk = col0 // V7X_LANES + b
        acc = None
        for j in range(ML_CONV):
            start = row0 + HALO - (ML_CONV - 1) + j
            tap = jnp.concatenate(
                [qk_ref[blk, pl.ds(start + s, groups, stride=V7X_SUBLANES), :] for s in range(V7X_SUBLANES)], axis=0)
            term = conv_ref[j:j + 1, blk * V7X_LANES:(blk + 1) * V7X_LANES] * tap
            acc = term if acc is None else acc + term
        act = jax.nn.silu(acc) * scale if scale != 1.0 else jax.nn.silu(acc)
        for s in range(V7X_SUBLANES):
            act_ref[blk, pl.ds(row0 + s, groups, stride=V7X_SUBLANES), :] = act[s * groups:(s + 1) * groups, :]
        outs.append(act_ref[blk, row0:row0 + L, :])
    return jnp.concatenate(outs, axis=1)


def _mlstm_kernel(n_in, n_cast, *refs):
    ins, cast_in, o_ref, cast_out, scratch = _split_refs(refs, n_in, n_cast)
    (x_ref, win_ref, bin_ref, wgc_ref, wgr_ref, bgc_ref, bgr_ref, conv_ref, ng_ref,
     wout_ref, lng_ref, lnb_ref) = ins
    qk_ref, act_ref, ct_ref, n_ref, m_ref, hb_ref = scratch
    L = ML_ROWS
    HALO = V7X_SUBLANES
    n_chunks = x_ref.shape[0] // L
    n_blk = WCHUNK // V7X_LANES

    @pl.when(pl.program_id(0) == 0)
    def _init():
        qk_ref[:, 0:HALO, :] = jnp.zeros((2 * ML_QK // V7X_LANES, HALO, V7X_LANES), F32)
        ct_ref[...] = jnp.zeros_like(ct_ref)
        n_ref[...] = jnp.zeros_like(n_ref)
        m_ref[...] = jnp.zeros_like(m_ref)

    xbs = [x_ref[c * L:(c + 1) * L, :].astype(BF16) for c in range(n_chunks)]

    ri = lax.broadcasted_iota(jnp.int32, (L, L), 0)
    ci = lax.broadcasted_iota(jnp.int32, (L, L), 1)
    causal = ri >= ci
    lower = causal.astype(BF16)
    upper = (ri <= ci).astype(BF16)

    def qk_piece(c, piece):
        pre = _dot(xbs[c], win_ref[piece]) + bin_ref[:, piece * WCHUNK:(piece + 1) * WCHUNK]
        for b in range(n_blk):
            qk_ref[piece * n_blk + b, HALO + c * L:HALO + (c + 1) * L, :] = pre[:, b * V7X_LANES:(b + 1) * V7X_LANES]

    def gates(c):
        g_col = _dot(xbs[c], wgc_ref[...]) + bgc_ref[...]
        g_row = _dot_nt(wgr_ref[...], xbs[c]) + bgr_ref[...]
        fh, fm, fl = _split3(_log_sigmoid(g_col))
        bcum_col = _dot(lower, fh) + _dot(lower, fm) + _dot(lower, fl)
        fh, fm, fl = _split3(_log_sigmoid(g_row))
        bcum_row = _dot(fh, upper) + _dot(fm, upper) + _dot(fl, upper)
        return g_col, g_row, bcum_col, bcum_row

    def head(c, h, gate_vals):
        g_col, g_row, bcum_col, bcum_row = gate_vals
        xb = xbs[c]
        q = _conv_silu(qk_ref, act_ref, conv_ref, c * L, h * ML_DQK, 1.0)
        k = _conv_silu(qk_ref, act_ref, conv_ref, c * L, ML_QK + h * ML_DQK, ML_DQK ** -0.5)
        qb = q.astype(BF16)
        kb = k.astype(BF16)
        vlo = 2 * ML_QK + h * ML_DV
        v = _dot(xb, win_ref[vlo // WCHUNK]) + bin_ref[:, vlo:vlo + ML_DV]
        vb = v.astype(BF16)

        i_col = g_col[:, h:h + 1]
        b_col = bcum_col[:, ML_HEADS + h:ML_HEADS + h + 1]
        i_row = g_row[h:h + 1, :]
        b_row = bcum_row[ML_HEADS + h:ML_HEADS + h + 1, :]
        m_prev = m_ref[h][:, 0:1]
        ct = ct_ref[h]
        n_prev = n_ref[h]

        d_log = jnp.where(causal, b_col - b_row + i_row, -jnp.inf)
        inter = b_col + m_prev
        m_t = jnp.maximum(inter, jnp.max(d_log, axis=-1, keepdims=True))
        s_mat = _dot_nt(qb, kb) * jnp.exp(d_log - m_t)
        w_inter = jnp.exp(inter - m_t)
        num = _dot(s_mat.astype(BF16), vb) + w_inter * _dot(qb, ct.astype(BF16))
        den = jnp.sum(s_mat, axis=-1, keepdims=True) + w_inter * jnp.sum(q * n_prev, axis=-1, keepdims=True)
        hh = num / jnp.maximum(jnp.abs(den), jnp.exp(-m_t))

        b_last = b_col[L - 1:L, :]
        w_log = b_last - b_col + i_col
        m_new = jnp.maximum(b_last + m_prev, jnp.max(w_log, axis=0, keepdims=True))
        decay = jnp.exp(b_last + m_prev - m_new)
        kw = k * jnp.exp(w_log - m_new)
        ct_ref[h] = decay * ct + _dot_tn(kw.astype(BF16), vb)
        n_ref[h] = decay * n_prev + jnp.sum(kw, axis=0, keepdims=True)
        m_ref[h] = jnp.broadcast_to(m_new, m_ref.shape[1:])

        mu = jnp.mean(hh, axis=-1, keepdims=True)
        hc = hh - mu
        var = jnp.mean(hc * hc, axis=-1, keepdims=True)
        hn = hc * lax.rsqrt(var + LN_EPS) * ng_ref[:, h * ML_DV:(h + 1) * ML_DV]
        zlo = 2 * ML_QK + ML_INNER + h * ML_DV
        z = _dot(xb, win_ref[zlo // WCHUNK]) + bin_ref[:, zlo:zlo + ML_DV]
        hb_ref[c * L:(c + 1) * L, h * ML_DV:(h + 1) * ML_DV] = (hn * jax.nn.sigmoid(z)).astype(BF16)

    def out_half(c, half):
        return _dot(hb_ref[c * L:(c + 1) * L, :], wout_ref[half])

    def finish(c, halves):
        rows = slice(c * L, (c + 1) * L)
        y = jnp.concatenate(halves, axis=1)
        o_ref[rows, :] = _layer_norm(DN_ALPHA * x_ref[rows, :] + y, lng_ref[1:2, :], lnb_ref[1:2, :])

    n_pieces = 2 * ML_QK // WCHUNK
    for piece in range(n_pieces):
        qk_piece(0, piece)
    _run_casts(cast_in, cast_out)
    gate_vals = [gates(c) for c in range(n_chunks)]
    for c in range(n_chunks):
        halves = []
        for h in range(ML_HEADS):
            head(c, h, gate_vals[c])
            if c + 1 < n_chunks:
                qk_piece(c + 1, h)
            if c > 0 and h < 2:
                halves.append(out_half(c - 1, h))
                if h == 1:
                    finish(c - 1, halves)
    finish(n_chunks - 1, [out_half(n_chunks - 1, 0), out_half(n_chunks - 1, 1)])

    qk_ref[:, 0:HALO, :] = qk_ref[:, n_chunks * L:n_chunks * L + HALO, :]


def _mlstm_call(x, w_in, b_in, wg_col, wg_row, bg_col, bg_row, conv_w, norm_g, w_out, ln_g, ln_b, layer, j,
                cast_weights):
    S, D = x.shape
    L = ML_ROWS
    main_specs = [
        _resident((ML_MAIN // WCHUNK, D, WCHUNK), (0, 0, 0)),
        _resident((None, 1, ML_MAIN), (j, 0, 0)),
        _resident((None, D, 2 * ML_HEADS), (j, 0, 0)),
        _resident((None, 4 * ML_HEADS, D), (j, 0, 0)),
        _resident((None, 1, 2 * ML_HEADS), (j, 0, 0)),
        _resident((None, 4 * ML_HEADS, 1), (j, 0, 0)),
        _resident((None, ML_CONV, 2 * ML_QK), (j, 0, 0)),
        _resident((None, 1, ML_INNER), (j, 0, 0)),
        _resident((2, ML_INNER, D // 2), (0, 0, 0)),
        _resident((None, 4, D), (layer, 0, 0)),
        _resident((None, 4, D), (layer, 0, 0)),
    ]
    main_args = [w_in, b_in, wg_col, wg_row, bg_col, bg_row, conv_w, norm_g, w_out, ln_g, ln_b]
    scratch = [
        pltpu.VMEM((2 * ML_QK // V7X_LANES, V7X_SUBLANES + ML_TILE, V7X_LANES), F32),
        pltpu.VMEM((2 * ML_QK // V7X_LANES, ML_TILE, V7X_LANES), F32),
        pltpu.VMEM((ML_HEADS, ML_DQK, ML_DV), F32),
        pltpu.VMEM((ML_HEADS, 1, ML_DQK), F32),
        pltpu.VMEM((ML_HEADS, 1, V7X_LANES), F32),
        pltpu.VMEM((ML_TILE, ML_INNER), BF16),
    ]
    return _call(_mlstm_kernel, "mlstm", x, pl.BlockSpec((ML_TILE, D), lambda t: (t, 0)),
                 main_specs, main_args, cast_weights, scratch)


def _gmlp_kernel(n_in, n_cast, *refs):
    ins, cast_in, o_ref, cast_out, (v_ref, u_ref, y_ref) = _split_refs(refs, n_in, n_cast)
    x_ref, win_ref, bin_ref, vng_ref, vnb_ref, ws_ref, bst_ref, wout_ref, lng_ref, lnb_ref = ins
    xb = x_ref[...].astype(BF16)
    for g in range(GM_GROUPS):
        lo = g * GM_GDIM
        vlo = GM_WIDTH + lo
        v_ref[:, lo:lo + GM_GDIM] = _gelu(_dot(xb, win_ref[:, vlo:vlo + GM_GDIM]) + bin_ref[:, vlo:vlo + GM_GDIM])
    _run_casts(cast_in, cast_out)
    for g in range(GM_GROUPS):
        lo = g * GM_GDIM
        u_ref[:, lo:lo + GM_GDIM] = _gelu(_dot(xb, win_ref[:, lo:lo + GM_GDIM]) + bin_ref[:, lo:lo + GM_GDIM])
    v = v_ref[...]
    mu = jnp.mean(v, axis=-1, keepdims=True)
    vc = v - mu
    rstd = lax.rsqrt(jnp.mean(vc * vc, axis=-1, keepdims=True) + LN_EPS)

    blk_r = lax.broadcasted_iota(jnp.int32, (GM_BLOCK, GM_BLOCK), 0) // CHUNK
    blk_c = lax.broadcasted_iota(jnp.int32, (GM_BLOCK, GM_BLOCK), 1) // CHUNK
    chunk_causal = blk_r >= blk_c
    for g in range(GM_GROUPS):
        lo = g * GM_GDIM
        vn = ((v_ref[:, lo:lo + GM_GDIM] - mu) * rstd * vng_ref[:, lo:lo + GM_GDIM]
              + vnb_ref[:, lo:lo + GM_GDIM]).astype(BF16)
        wsm = jnp.where(chunk_causal, ws_ref[g], 0.0).astype(BF16)
        bias = bst_ref[:, g:g + 1]
        for nb in range(GM_ROWS // GM_BLOCK):
            r0 = nb * GM_BLOCK
            s = _dot(wsm, vn[r0:r0 + GM_BLOCK, :]) + bias
            y_ref[r0:r0 + GM_BLOCK, lo:lo + GM_GDIM] = (u_ref[r0:r0 + GM_BLOCK, lo:lo + GM_GDIM] * s).astype(BF16)
    for r in range(GM_ROWS // GM_SUB):
        rs = slice(r * GM_SUB, (r + 1) * GM_SUB)
        y = _dot(y_ref[rs, :], wout_ref[...])
        o_ref[rs, :] = _layer_norm(DN_ALPHA * x_ref[rs, :] + y, lng_ref[1:2, :], lnb_ref[1:2, :])


def _gmlp_call(x, w_in, b_in, vn_g, vn_b, ws, bs_t, w_out, ln_g, ln_b, layer, j, cast_weights):
    S, D = x.shape
    main_specs = [
        _resident((D, 2 * GM_WIDTH), (0, 0)),
        _resident((None, 1, 2 * GM_WIDTH), (j, 0, 0)),
        _resident((None, 1, GM_WIDTH), (j, 0, 0)),
        _resident((None, 1, GM_WIDTH), (j, 0, 0)),
        _resident((None, GM_GROUPS, GM_BLOCK, GM_BLOCK), (j, 0, 0, 0)),
        _resident((None, GM_BLOCK, GM_GROUPS), (j, 0, 0)),
        _resident((GM_WIDTH, D), (0, 0)),
        _resident((None, 4, D), (layer, 0, 0)),
        _resident((None, 4, D), (layer, 0, 0)),
    ]
    main_args = [w_in, b_in, vn_g, vn_b, ws, bs_t, w_out, ln_g, ln_b]
    scratch = [
        pltpu.VMEM((GM_ROWS, GM_WIDTH), F32),
        pltpu.VMEM((GM_ROWS, GM_WIDTH), F32),
        pltpu.VMEM((GM_ROWS, GM_WIDTH), BF16),
    ]
    return _call(_gmlp_kernel, "gmlp", x, pl.BlockSpec((GM_ROWS, D), lambda t: (t, 0)),
                 main_specs, main_args, cast_weights, scratch)


def kernel(x, p, ln_g, ln_b, ffn1_wgu, ffn1_wd, ffn2_wgu, ffn2_wd, ml_w_in, ml_b_in, ml_conv, ml_norm_g, ml_w_out, gm_w_in, gm_b_in, gm_vn_g, gm_vn_b, gm_ws, gm_bs, gm_w_out, ple_wp, ple_wg, ple_bg):
    B, S, D = x.shape
    assert (B, S, D) == (1, SEQ, D_MODEL)
    xs = x.reshape(S, D)
    ps = p.reshape(DEPTH, S, PLE_DIM)

    ple_bg_r = ple_bg[:, None, :]
    ml_b_main = ml_b_in[:, None, :ML_MAIN]
    ml_wg_col = ml_w_in[:, :, ML_MAIN:].astype(BF16)
    ml_wg_row = jnp.pad(jnp.swapaxes(ml_w_in[:, :, ML_MAIN:], 1, 2),
                        ((0, 0), (0, 2 * ML_HEADS), (0, 0))).astype(BF16)
    ml_bg_col = ml_b_in[:, None, ML_MAIN:]
    ml_bg_row = jnp.pad(ml_b_in[:, ML_MAIN:], ((0, 0), (0, 2 * ML_HEADS)))[:, :, None]
    ml_norm_g_r = ml_norm_g[:, None, :]
    ml_w_in_t = jnp.swapaxes(ml_w_in, 1, 2)
    gm_b_in_r = gm_b_in[:, None, :]
    gm_vn_g_r, gm_vn_b_r = gm_vn_g[:, None, :], gm_vn_b[:, None, :]
    gm_bs_t = jnp.swapaxes(gm_bs, 1, 2)

    ffn1_w = (ffn1_wgu[0].astype(BF16), ffn1_wd[0].astype(BF16))
    for i in range(DEPTH):
        j = i // 2
        if i % 2 == 0:
            mixer_f32 = [("transposed", ml_w_in_t, j), ("halves", ml_w_out, j)]
        else:
            mixer_f32 = [("plain", gm_w_in, j), ("plain", gm_w_out, j)]
        ffn2_rest_f32 = [("plain", ffn2_wd, i), ("plain", ple_wg, i), ("plain", ple_wp, i)]
        xs, cast_w = _ffn_call(xs, *ffn1_w, ln_g, ln_b, i, 0, mixer_f32 + ffn2_rest_f32)
        mixer_w, ffn2_rest_w = cast_w[:2], cast_w[2:]
        ffn2_f32 = [("plain", ffn2_wgu, i)]
        if i % 2 == 0:
            xs, ffn2_w = _mlstm_call(xs, mixer_w[0], ml_b_main, ml_wg_col, ml_wg_row, ml_bg_col, ml_bg_row,
                                     ml_conv, ml_norm_g_r, mixer_w[1], ln_g, ln_b, i, j, ffn2_f32)
        else:
            xs, ffn2_w = _gmlp_call(xs, mixer_w[0], gm_b_in_r, gm_vn_g_r, gm_vn_b_r, gm_ws, gm_bs_t,
                                    mixer_w[1], ln_g, ln_b, i, j, ffn2_f32)
        next_f32 = [("plain", ffn1_wgu, i + 1), ("plain", ffn1_wd, i + 1)] if i + 1 < DEPTH else []
        xs, ffn1_w = _ffn_call(xs, ffn2_w[0], ffn2_rest_w[0], ln_g, ln_b, i, 2, next_f32,
                               ple=(ps, ffn2_rest_w[1], ple_bg_r, ffn2_rest_w[2]))
    return xs.reshape(B, S, D)
```

```python
import functools

import jax
import jax.numpy as jnp
from jax import lax
from jax.experimental import pallas as pl
from jax.experimental.pallas import tpu as pltpu

D_MODEL = 1024
SEQ = 16384
DEPTH = 4
CHUNK = 64
PLE_DIM = 256
FFN_HIDDEN = 2816
ML_HEADS = 4
ML_DQK = 256
ML_DV = 512
ML_INNER = ML_HEADS * ML_DV
ML_QK = ML_HEADS * ML_DQK
ML_MAIN = 2 * ML_QK + 2 * ML_INNER
ML_CONV = 4
GM_BLOCK = 128
GM_GROUPS = 8
GM_WIDTH = 2 * D_MODEL
GM_GDIM = GM_WIDTH // GM_GROUPS
DN_ALPHA = (2.0 * DEPTH) ** 0.25
LN_EPS = 1e-5

V7X_MXU_DIM = 256
V7X_SUBLANES = 8
V7X_LANES = 128
V7X_BF16_ROWS = 16
V7X_VMEM_BYTES = 64 * 1024 * 1024
VMEM_LIMIT_BYTES = V7X_VMEM_BYTES * 7 // 8

FFN_ROWS = 1024
FFN_PLE_ROWS = 1024
FFN_COLS = V7X_MXU_DIM
FFN_SUB = 256
ML_ROWS = 256
ML_TILE = 512
GM_ROWS = 1024
GM_SUB = 256
WCHUNK = 512

F32 = jnp.float32
BF16 = jnp.bfloat16


def _dot(a, b):
    return jnp.dot(a, b, preferred_element_type=F32)


def _dot_nt(a, b):
    return lax.dot_general(a, b, (((1,), (1,)), ((), ())), preferred_element_type=F32)


def _dot_tn(a, b):
    return lax.dot_general(a, b, (((0,), (0,)), ((), ())), preferred_element_type=F32)


def _layer_norm(z, g, b):
    mu = jnp.mean(z, axis=-1, keepdims=True)
    zc = z - mu
    var = jnp.mean(zc * zc, axis=-1, keepdims=True)
    return zc * lax.rsqrt(var + LN_EPS) * g + b


def _split3(a):
    hi = a.astype(BF16)
    r1 = a - hi.astype(F32)
    mid = r1.astype(BF16)
    lo = (r1 - mid.astype(F32)).astype(BF16)
    return hi, mid, lo


def _gelu(x):
    return 0.5 * x * (1.0 + lax.erf(x * (2.0 ** -0.5)))


def _log_sigmoid(x):
    return jnp.minimum(x, 0.0) - jnp.log1p(jnp.exp(-jnp.abs(x)))


def _resident(shape, index):
    return pl.BlockSpec(shape, lambda t: index, pipeline_mode=pl.Buffered(1))


def _cast_plan(weights, steps):
    in_specs, out_specs, out_shapes, args = [], [], [], []
    for kind, w, layer in weights:
        args.append(w)
        if kind == "transposed":
            _, N, K = w.shape
            last = N // WCHUNK - 1
            assert steps > last
            in_specs.append(pl.BlockSpec((None, WCHUNK, K),
                                         lambda t, layer=layer, last=last: (layer, jnp.minimum(t, last), 0)))
            out_specs.append(pl.BlockSpec((None, K, WCHUNK), lambda t, last=last: (jnp.minimum(t, last), 0, 0)))
            out_shapes.append(jax.ShapeDtypeStruct((last + 1, K, WCHUNK), BF16))
            continue
        _, K, N = w.shape
        hold = 1
        while K % (steps // hold) or (K // (steps // hold)) % V7X_BF16_ROWS:
            hold *= 2
        rows = K // (steps // hold)
        nblk = steps // hold

        def blk(t, hold=hold, nblk=nblk):
            return jnp.minimum(t // hold, nblk - 1)

        in_specs.append(pl.BlockSpec((None, rows, N), lambda t, layer=layer, blk=blk: (layer, blk(t), 0)))
        if kind == "halves":
            out_specs.append(pl.BlockSpec((2, rows, N // 2), lambda t, blk=blk: (0, blk(t), 0)))
            out_shapes.append(jax.ShapeDtypeStruct((2, K, N // 2), BF16))
        else:
            out_specs.append(pl.BlockSpec((rows, N), lambda t, blk=blk: (blk(t), 0)))
            out_shapes.append(jax.ShapeDtypeStruct((K, N), BF16))
    return in_specs, out_specs, out_shapes, args


def _split_refs(refs, n_in, n_cast):
    a, b, c = n_in, n_in + n_cast, n_in + 2 * n_cast + 1
    return refs[:a], refs[a:b], refs[b], refs[b + 1:c], refs[c:]


def _run_casts(cast_in, cast_out):
    for src, dst in zip(cast_in, cast_out):
        if len(dst.shape) == 3:
            half = dst.shape[2]
            dst[0] = src[:, :half].astype(BF16)
            dst[1] = src[:, half:].astype(BF16)
        elif dst.shape != src.shape:
            dst[...] = src[...].T.astype(BF16)
        else:
            dst[...] = src[...].astype(BF16)


def _call(kernel_fn, name, x, row_spec, main_specs, main_args, cast_weights, scratch_shapes, out_row_spec=None):
    S, D = x.shape
    steps = S // row_spec.block_shape[0]
    c_in, c_out, c_shapes, c_args = _cast_plan(cast_weights, steps)
    outs = pl.pallas_call(
        functools.partial(kernel_fn, 1 + len(main_args), len(c_args)),
        grid=(steps if out_row_spec is None else steps + 1,),
        in_specs=[row_spec] + main_specs + c_in,
        out_specs=[row_spec if out_row_spec is None else out_row_spec] + c_out,
        out_shape=[jax.ShapeDtypeStruct((S, D), F32)] + c_shapes,
        scratch_shapes=scratch_shapes,
        compiler_params=pltpu.CompilerParams(
            dimension_semantics=("arbitrary",), vmem_limit_bytes=VMEM_LIMIT_BYTES),
        name=name,
    )(x, *main_args, *c_args)
    return outs[0], outs[1:]


def _ffn_kernel(with_ple, ln_row, n_in, n_cast, *refs):
    ins, cast_in, o_ref, cast_out, (a_ref,) = _split_refs(refs, n_in, n_cast)
    if with_ple:
        x_ref, wgu_ref, wd_ref, lng_ref, lnb_ref, p_ref, wpg_ref, bpg_ref, wpp_ref = ins
    else:
        x_ref, wgu_ref, wd_ref, lng_ref, lnb_ref = ins
    _run_casts(cast_in, cast_out)
    xb = x_ref[...].astype(BF16)
    for c in range(FFN_HIDDEN // FFN_COLS):
        lo = c * FFN_COLS
        g = _dot(xb, wgu_ref[:, lo:lo + FFN_COLS])
        u = _dot(xb, wgu_ref[:, FFN_HIDDEN + lo:FFN_HIDDEN + lo + FFN_COLS])
        a_ref[:, lo:lo + FFN_COLS] = (jax.nn.silu(g) * u).astype(BF16)
    n_sub = a_ref.shape[0] // FFN_SUB

    def down(r):
        rs = slice(r * FFN_SUB, (r + 1) * FFN_SUB)
        y = _dot(a_ref[rs, :], wd_ref[...])
        return _layer_norm(DN_ALPHA * x_ref[rs, :] + 0.5 * y,
                           lng_ref[ln_row:ln_row + 1, :], lnb_ref[ln_row:ln_row + 1, :])

    x1 = down(0)
    for r in range(n_sub):
        rs = slice(r * FFN_SUB, (r + 1) * FFN_SUB)
        x1_next = down(r + 1) if r + 1 < n_sub else None
        if with_ple:
            gate = jax.nn.sigmoid(_dot(x1.astype(BF16), wpg_ref[...]) + bpg_ref[...])
            emb = _dot(p_ref[rs, :].astype(BF16), wpp_ref[...])
            x1 = _layer_norm(DN_ALPHA * x1 + gate * emb, lng_ref[3:4, :], lnb_ref[3:4, :])
        o_ref[rs, :] = x1
        x1 = x1_next


def _ffn_lag_kernel(ln_row, n_in, n_cast, *refs):
    ins, cast_in, o_ref, cast_out, (a_ref, o_scr) = _split_refs(refs, n_in, n_cast)
    x_ref, wgu_ref, wd_ref, lng_ref, lnb_ref = ins
    t = pl.program_id(0)
    last_step = pl.num_programs(0) - 1
    n_sub = a_ref.shape[0] // FFN_SUB
    done = (n_sub - 1) * FFN_SUB
    g_ln = lng_ref[ln_row:ln_row + 1, :]
    b_ln = lnb_ref[ln_row:ln_row + 1, :]

    def flush_previous_tile():
        o_ref[0:done, :] = o_scr[0:done, :]
        o_ref[done:, :] = _layer_norm(o_scr[done:, :], g_ln, b_ln)

    @pl.when(t == 0)
    def _init():
        o_scr[...] = jnp.zeros_like(o_scr)

    @pl.when(t < last_step)
    def _main():
        flush_previous_tile()
        _run_casts(cast_in, cast_out)
        xb = x_ref[...].astype(BF16)
        for c in range(FFN_HIDDEN // FFN_COLS):
            lo = c * FFN_COLS
            g = _dot(xb, wgu_ref[:, lo:lo + FFN_COLS])
            u = _dot(xb, wgu_ref[:, FFN_HIDDEN + lo:FFN_HIDDEN + lo + FFN_COLS])
            a_ref[:, lo:lo + FFN_COLS] = (jax.nn.silu(g) * u).astype(BF16)
        for r in range(n_sub):
            rs = slice(r * FFN_SUB, (r + 1) * FFN_SUB)
            z = DN_ALPHA * x_ref[rs, :] + 0.5 * _dot(a_ref[rs, :], wd_ref[...])
            o_scr[rs, :] = _layer_norm(z, g_ln, b_ln) if r + 1 < n_sub else z

    @pl.when(t == last_step)
    def _drain():
        flush_previous_tile()


def _ffn_call(x, wgu, wd, ln_g, ln_b, layer, ln_row, cast_weights, ple=None):
    S, D = x.shape
    F = FFN_HIDDEN
    tm = FFN_ROWS if ple is None else FFN_PLE_ROWS
    if ple is None:
        steps = S // tm
        specs = [_resident((D, 2 * F), (0, 0)), _resident((F, D), (0, 0)),
                 _resident((None, 4, D), (layer, 0, 0)), _resident((None, 4, D), (layer, 0, 0))]
        return _call(functools.partial(_ffn_lag_kernel, ln_row), "ffn", x,
                     pl.BlockSpec((tm, D), lambda t: (jnp.minimum(t, steps - 1), 0)), specs, [wgu, wd, ln_g, ln_b],
                     cast_weights, [pltpu.VMEM((tm, F), BF16), pltpu.VMEM((tm, D), F32)],
                     out_row_spec=pl.BlockSpec((tm, D), lambda t: (jnp.maximum(t - 1, 0), 0)))
    main_specs = [
        _resident((D, 2 * F), (0, 0)),
        _resident((F, D), (0, 0)),
        _resident((None, 4, D), (layer, 0, 0)),
        _resident((None, 4, D), (layer, 0, 0)),
    ]
    main_args = [wgu, wd, ln_g, ln_b]
    if ple is not None:
        p, wpg, bpg, wpp = ple
        main_specs += [
            pl.BlockSpec((None, tm, PLE_DIM), lambda t: (layer, t, 0)),
            _resident((D, D), (0, 0)),
            _resident((None, 1, D), (layer, 0, 0)),
            _resident((PLE_DIM, D), (0, 0)),
        ]
        main_args += [p, wpg, bpg, wpp]
    return _call(functools.partial(_ffn_kernel, ple is not None, ln_row),
                 "ffn_ple" if ple is not None else "ffn",
                 x, pl.BlockSpec((tm, D), lambda t: (t, 0)), main_specs, main_args, cast_weights,
                 [pltpu.VMEM((tm, F), BF16)])


def _conv_silu(qk_ref, act_ref, conv_ref, row0, col0, scale):
    L = ML_ROWS
    HALO = V7X_SUBLANES
    groups = L // V7X_SUBLANES
    outs = []
    for b in range(ML_DQK // V7X_LANES):
        blk = col0 // V7X_LANES + b
        acc = None
        for j in range(ML_CONV):
            start = row0 + HALO - (ML_CONV - 1) + j
            tap = jnp.concatenate(
                [qk_ref[blk, pl.ds(start + s, groups, stride=V7X_SUBLANES), :] for s in range(V7X_SUBLANES)], axis=0)
            term = conv_ref[j:j + 1, blk * V7X_LANES:(blk + 1) * V7X_LANES] * tap
            acc = term if acc is None else acc + term
        act = jax.nn.silu(acc) * scale if scale != 1.0 else jax.nn.silu(acc)
        for s in range(V7X_SUBLANES):
            act_ref[blk, pl.ds(row0 + s, groups, stride=V7X_SUBLANES), :] = act[s * groups:(s + 1) * groups, :]
        outs.append(act_ref[blk, row0:row0 + L, :])
    return jnp.concatenate(outs, axis=1)


def _mlstm_kernel(n_in, n_cast, *refs):
    ins, cast_in, o_ref, cast_out, scratch = _split_refs(refs, n_in, n_cast)
    (x_ref, win_ref, bin_ref, wgc_ref, wgr_ref, bgc_ref, bgr_ref, conv_ref, ng_ref,
     wout_ref, lng_ref, lnb_ref) = ins
    qk_ref, act_ref, ct_ref, n_ref, m_ref, hb_ref = scratch
    L = ML_ROWS
    HALO = V7X_SUBLANES
    n_chunks = x_ref.shape[0] // L
    n_blk = WCHUNK // V7X_LANES

    @pl.when(pl.program_id(0) == 0)
    def _init():
        qk_ref[:, 0:HALO, :] = jnp.zeros((2 * ML_QK // V7X_LANES, HALO, V7X_LANES), F32)
        ct_ref[...] = jnp.zeros_like(ct_ref)
        n_ref[...] = jnp.zeros_like(n_ref)
        m_ref[...] = jnp.zeros_like(m_ref)

    _run_casts(cast_in, cast_out)
    xbs = [x_ref[c * L:(c + 1) * L, :].astype(BF16) for c in range(n_chunks)]

    ri = lax.broadcasted_iota(jnp.int32, (L, L), 0)
    ci = lax.broadcasted_iota(jnp.int32, (L, L), 1)
    causal = ri >= ci
    lower = causal.astype(BF16)
    upper = (ri <= ci).astype(BF16)

    def qk_piece(c, piece):
        pre = _dot(xbs[c], win_ref[piece]) + bin_ref[:, piece * WCHUNK:(piece + 1) * WCHUNK]
        for b in range(n_blk):
            qk_ref[piece * n_blk + b, HALO + c * L:HALO + (c + 1) * L, :] = pre[:, b * V7X_LANES:(b + 1) * V7X_LANES]

    def gates(c):
        g_col = _dot(xbs[c], wgc_ref[...]) + bgc_ref[...]
        g_row = _dot_nt(wgr_ref[...], xbs[c]) + bgr_ref[...]
        fh, fm, fl = _split3(_log_sigmoid(g_col))
        bcum_col = _dot(lower, fh) + _dot(lower, fm) + _dot(lower, fl)
        fh, fm, fl = _split3(_log_sigmoid(g_row))
        bcum_row = _dot(fh, upper) + _dot(fm, upper) + _dot(fl, upper)
        return g_col, g_row, bcum_col, bcum_row

    def head(c, h, gate_vals):
        g_col, g_row, bcum_col, bcum_row = gate_vals
        xb = xbs[c]
        q = _conv_silu(qk_ref, act_ref, conv_ref, c * L, h * ML_DQK, 1.0)
        k = _conv_silu(qk_ref, act_ref, conv_ref, c * L, ML_QK + h * ML_DQK, ML_DQK ** -0.5)
        qb = q.astype(BF16)
        kb = k.astype(BF16)
        vlo = 2 * ML_QK + h * ML_DV
        v = _dot(xb, win_ref[vlo // WCHUNK]) + bin_ref[:, vlo:vlo + ML_DV]
        vb = v.astype(BF16)

        i_col = g_col[:, h:h + 1]
        b_col = bcum_col[:, ML_HEADS + h:ML_HEADS + h + 1]
        i_row = g_row[h:h + 1, :]
        b_row = bcum_row[ML_HEADS + h:ML_HEADS + h + 1, :]
        m_prev = m_ref[h][:, 0:1]
        ct = ct_ref[h]
        n_prev = n_ref[h]

        d_log = jnp.where(causal, b_col - b_row + i_row, -jnp.inf)
        inter = b_col + m_prev
        m_t = jnp.maximum(inter, jnp.max(d_log, axis=-1, keepdims=True))
        s_mat = _dot_nt(qb, kb) * jnp.exp(d_log - m_t)
        w_inter = jnp.exp(inter - m_t)
        num = _dot(s_mat.astype(BF16), vb) + w_inter * _dot(qb, ct.astype(BF16))
        den = jnp.sum(s_mat, axis=-1, keepdims=True) + w_inter * jnp.sum(q * n_prev, axis=-1, keepdims=True)
        hh = num / jnp.maximum(jnp.abs(den), jnp.exp(-m_t))

        b_last = b_col[L - 1:L, :]
        w_log = b_last - b_col + i_col
        m_new = jnp.maximum(b_last + m_prev, jnp.max(w_log, axis=0, keepdims=True))
        decay = jnp.exp(b_last + m_prev - m_new)
        kw = k * jnp.exp(w_log - m_new)
        ct_ref[h] = decay * ct + _dot_tn(kw.astype(BF16), vb)
        n_ref[h] = decay * n_prev + jnp.sum(kw, axis=0, keepdims=True)
        m_ref[h] = jnp.broadcast_to(m_new, m_ref.shape[1:])

        mu = jnp.mean(hh, axis=-1, keepdims=True)
        hc = hh - mu
        var = jnp.mean(hc * hc, axis=-1, keepdims=True)
        hn = hc * lax.rsqrt(var + LN_EPS) * ng_ref[:, h * ML_DV:(h + 1) * ML_DV]
        zlo = 2 * ML_QK + ML_INNER + h * ML_DV
        z = _dot(xb, win_ref[zlo // WCHUNK]) + bin_ref[:, zlo:zlo + ML_DV]
        hb_ref[c * L:(c + 1) * L, h * ML_DV:(h + 1) * ML_DV] = (hn * jax.nn.sigmoid(z)).astype(BF16)

    def out_half(c, half):
        return _dot(hb_ref[c * L:(c + 1) * L, :], wout_ref[half])

    def finish(c, halves):
        rows = slice(c * L, (c + 1) * L)
        y = jnp.concatenate(halves, axis=1)
        o_ref[rows, :] = _layer_norm(DN_ALPHA * x_ref[rows, :] + y, lng_ref[1:2, :], lnb_ref[1:2, :])

    n_pieces = 2 * ML_QK // WCHUNK
    for piece in range(n_pieces):
        qk_piece(0, piece)
    gate_vals = [gates(c) for c in range(n_chunks)]
    for c in range(n_chunks):
        halves = []
        for h in range(ML_HEADS):
            head(c, h, gate_vals[c])
            if c + 1 < n_chunks:
                qk_piece(c + 1, h)
            if c > 0 and h < 2:
                halves.append(out_half(c - 1, h))
                if h == 1:
                    finish(c - 1, halves)
    finish(n_chunks - 1, [out_half(n_chunks - 1, 0), out_half(n_chunks - 1, 1)])

    qk_ref[:, 0:HALO, :] = qk_ref[:, n_chunks * L:n_chunks * L + HALO, :]


def _mlstm_call(x, w_in, b_in, wg_col, wg_row, bg_col, bg_row, conv_w, norm_g, w_out, ln_g, ln_b, layer, j,
                cast_weights):
    S, D = x.shape
    L = ML_ROWS
    main_specs = [
        _resident((ML_MAIN // WCHUNK, D, WCHUNK), (0, 0, 0)),
        _resident((None, 1, ML_MAIN), (j, 0, 0)),
        _resident((None, D, 2 * ML_HEADS), (j, 0, 0)),
        _resident((None, 4 * ML_HEADS, D), (j, 0, 0)),
        _resident((None, 1, 2 * ML_HEADS), (j, 0, 0)),
        _resident((None, 4 * ML_HEADS, 1), (j, 0, 0)),
        _resident((None, ML_CONV, 2 * ML_QK), (j, 0, 0)),
        _resident((None, 1, ML_INNER), (j, 0, 0)),
        _resident((2, ML_INNER, D // 2), (0, 0, 0)),
        _resident((None, 4, D), (layer, 0, 0)),
        _resident((None, 4, D), (layer, 0, 0)),
    ]
    main_args = [w_in, b_in, wg_col, wg_row, bg_col, bg_row, conv_w, norm_g, w_out, ln_g, ln_b]
    scratch = [
        pltpu.VMEM((2 * ML_QK // V7X_LANES, V7X_SUBLANES + ML_TILE, V7X_LANES), F32),
        pltpu.VMEM((2 * ML_QK // V7X_LANES, ML_TILE, V7X_LANES), F32),
        pltpu.VMEM((ML_HEADS, ML_DQK, ML_DV), F32),
        pltpu.VMEM((ML_HEADS, 1, ML_DQK), F32),
        pltpu.VMEM((ML_HEADS, 1, 128), F32),
        pltpu.VMEM((ML_TILE, ML_INNER), BF16),
    ]
    return _call(_mlstm_kernel, "mlstm", x, pl.BlockSpec((ML_TILE, D), lambda t: (t, 0)),
                 main_specs, main_args, cast_weights, scratch)


def _gmlp_kernel(n_in, n_cast, *refs):
    ins, cast_in, o_ref, cast_out, (v_ref, u_ref, y_ref) = _split_refs(refs, n_in, n_cast)
    x_ref, win_ref, bin_ref, vng_ref, vnb_ref, ws_ref, bst_ref, wout_ref, lng_ref, lnb_ref = ins
    _run_casts(cast_in, cast_out)
    xb = x_ref[...].astype(BF16)
    for g in range(GM_GROUPS):
        lo = g * GM_GDIM
        vlo = GM_WIDTH + lo
        v_ref[:, lo:lo + GM_GDIM] = _gelu(_dot(xb, win_ref[:, vlo:vlo + GM_GDIM]) + bin_ref[:, vlo:vlo + GM_GDIM])
    for g in range(GM_GROUPS):
        lo = g * GM_GDIM
        u_ref[:, lo:lo + GM_GDIM] = _gelu(_dot(xb, win_ref[:, lo:lo + GM_GDIM]) + bin_ref[:, lo:lo + GM_GDIM])
    v = v_ref[...]
    mu = jnp.mean(v, axis=-1, keepdims=True)
    vc = v - mu
    rstd = lax.rsqrt(jnp.mean(vc * vc, axis=-1, keepdims=True) + LN_EPS)

    blk_r = lax.broadcasted_iota(jnp.int32, (GM_BLOCK, GM_BLOCK), 0) // CHUNK
    blk_c = lax.broadcasted_iota(jnp.int32, (GM_BLOCK, GM_BLOCK), 1) // CHUNK
    chunk_causal = blk_r >= blk_c
    for g in range(GM_GROUPS):
        lo = g * GM_GDIM
        vn = ((v_ref[:, lo:lo + GM_GDIM] - mu) * rstd * vng_ref[:, lo:lo + GM_GDIM]
              + vnb_ref[:, lo:lo + GM_GDIM]).astype(BF16)
        wsm = jnp.where(chunk_causal, ws_ref[g], 0.0).astype(BF16)
        bias = bst_ref[:, g:g + 1]
        for nb in range(GM_ROWS // GM_BLOCK):
            r0 = nb * GM_BLOCK
            s = _dot(wsm, vn[r0:r0 + GM_BLOCK, :]) + bias
            y_ref[r0:r0 + GM_BLOCK, lo:lo + GM_GDIM] = (u_ref[r0:r0 + GM_BLOCK, lo:lo + GM_GDIM] * s).astype(BF16)
    for r in range(GM_ROWS // GM_SUB):
        rs = slice(r * GM_SUB, (r + 1) * GM_SUB)
        y = _dot(y_ref[rs, :], wout_ref[...])
        o_ref[rs, :] = _layer_norm(DN_ALPHA * x_ref[rs, :] + y, lng_ref[1:2, :], lnb_ref[1:2, :])


def _gmlp_call(x, w_in, b_in, vn_g, vn_b, ws, bs_t, w_out, ln_g, ln_b, layer, j, cast_weights):
    S, D = x.shape
    main_specs = [
        _resident((D, 2 * GM_WIDTH), (0, 0)),
        _resident((None, 1, 2 * GM_WIDTH), (j, 0, 0)),
        _resident((None, 1, GM_WIDTH), (j, 0, 0)),
        _resident((None, 1, GM_WIDTH), (j, 0, 0)),
        _resident((None, GM_GROUPS, GM_BLOCK, GM_BLOCK), (j, 0, 0, 0)),
        _resident((None, GM_BLOCK, GM_GROUPS), (j, 0, 0)),
        _resident((GM_WIDTH, D), (0, 0)),
        _resident((None, 4, D), (layer, 0, 0)),
        _resident((None, 4, D), (layer, 0, 0)),
    ]
    main_args = [w_in, b_in, vn_g, vn_b, ws, bs_t, w_out, ln_g, ln_b]
    scratch = [
        pltpu.VMEM((GM_ROWS, GM_WIDTH), F32),
        pltpu.VMEM((GM_ROWS, GM_WIDTH), F32),
        pltpu.VMEM((GM_ROWS, GM_WIDTH), BF16),
    ]
    return _call(_gmlp_kernel, "gmlp", x, pl.BlockSpec((GM_ROWS, D), lambda t: (t, 0)),
                 main_specs, main_args, cast_weights, scratch)


def kernel(x, p, ln_g, ln_b, ffn1_wgu, ffn1_wd, ffn2_wgu, ffn2_wd, ml_w_in, ml_b_in, ml_conv, ml_norm_g, ml_w_out, gm_w_in, gm_b_in, gm_vn_g, gm_vn_b, gm_ws, gm_bs, gm_w_out, ple_wp, ple_wg, ple_bg):
    B, S, D = x.shape
    assert (B, S, D) == (1, SEQ, D_MODEL)
    xs = x.reshape(S, D)
    ps = p.reshape(DEPTH, S, PLE_DIM)

    ple_bg_r = ple_bg[:, None, :]
    ml_b_main = ml_b_in[:, None, :ML_MAIN]
    ml_wg_col = ml_w_in[:, :, ML_MAIN:].astype(BF16)
    ml_wg_row = jnp.pad(jnp.swapaxes(ml_w_in[:, :, ML_MAIN:], 1, 2),
                        ((0, 0), (0, 2 * ML_HEADS), (0, 0))).astype(BF16)
    ml_bg_col = ml_b_in[:, None, ML_MAIN:]
    ml_bg_row = jnp.pad(ml_b_in[:, ML_MAIN:], ((0, 0), (0, 2 * ML_HEADS)))[:, :, None]
    ml_norm_g_r = ml_norm_g[:, None, :]
    ml_w_in_t = jnp.swapaxes(ml_w_in, 1, 2)
    gm_b_in_r = gm_b_in[:, None, :]
    gm_vn_g_r, gm_vn_b_r = gm_vn_g[:, None, :], gm_vn_b[:, None, :]
    gm_bs_t = jnp.swapaxes(gm_bs, 1, 2)

    ffn1_w = (ffn1_wgu[0].astype(BF16), ffn1_wd[0].astype(BF16))
    for i in range(DEPTH):
        j = i // 2
        if i % 2 == 0:
            mixer_f32 = [("transposed", ml_w_in_t, j), ("halves", ml_w_out, j)]
        else:
            mixer_f32 = [("plain", gm_w_in, j), ("plain", gm_w_out, j)]
        ffn2_rest_f32 = [("plain", ffn2_wd, i), ("plain", ple_wg, i), ("plain", ple_wp, i)]
        xs, cast_w = _ffn_call(xs, *ffn1_w, ln_g, ln_b, i, 0, mixer_f32 + ffn2_rest_f32)
        mixer_w, ffn2_rest_w = cast_w[:2], cast_w[2:]
        ffn2_f32 = [("plain", ffn2_wgu, i)]
        if i % 2 == 0:
            xs, ffn2_w = _mlstm_call(xs, mixer_w[0], ml_b_main, ml_wg_col, ml_wg_row, ml_bg_col, ml_bg_row,
                                     ml_conv, ml_norm_g_r, mixer_w[1], ln_g, ln_b, i, j, ffn2_f32)
        else:
            xs, ffn2_w = _gmlp_call(xs, mixer_w[0], gm_b_in_r, gm_vn_g_r, gm_vn_b_r, gm_ws, gm_bs_t,
                                    mixer_w[1], ln_g, ln_b, i, j, ffn2_f32)
        next_f32 = [("plain", ffn1_wgu, i + 1), ("plain", ffn1_wd, i + 1)] if i + 1 < DEPTH else []
        xs, ffn1_w = _ffn_call(xs, ffn2_w[0], ffn2_rest_w[0], ln_g, ln_b, i, 2, next_f32,
                               ple=(ps, ffn2_rest_w[1], ple_bg_r, ffn2_rest_w[2]))
    return xs.reshape(B, S, D)
```

```python
import functools

import jax
import jax.numpy as jnp
from jax import lax
from jax.experimental import pallas as pl
from jax.experimental.pallas import tpu as pltpu

D_MODEL = 1024
SEQ = 16384
DEPTH = 4
CHUNK = 64
PLE_DIM = 256
FFN_HIDDEN = 2816
ML_HEADS = 4
ML_DQK = 256
ML_DV = 512
ML_INNER = ML_HEADS * ML_DV
ML_QK = ML_HEADS * ML_DQK
ML_MAIN = 2 * ML_QK + 2 * ML_INNER
ML_CONV = 4
GM_BLOCK = 128
GM_GROUPS = 8
GM_WIDTH = 2 * D_MODEL
GM_GDIM = GM_WIDTH // GM_GROUPS
DN_ALPHA = (2.0 * DEPTH) ** 0.25
LN_EPS = 1e-5

V7X_MXU_DIM = 256
V7X_SUBLANES = 8
V7X_LANES = 128
V7X_BF16_ROWS = 16
V7X_VMEM_BYTES = 64 * 1024 * 1024
VMEM_LIMIT_BYTES = V7X_VMEM_BYTES * 7 // 8

FFN_ROWS = 1024
FFN_PLE_ROWS = 1024
FFN_COLS = V7X_MXU_DIM
FFN_SUB = 256
ML_ROWS = 256
ML_TILE = 512
GM_ROWS = 1024
GM_SUB = 256
WCHUNK = 512

F32 = jnp.float32
BF16 = jnp.bfloat16


def _dot(a, b):
    return jnp.dot(a, b, preferred_element_type=F32)


def _dot_nt(a, b):
    return lax.dot_general(a, b, (((1,), (1,)), ((), ())), preferred_element_type=F32)


def _dot_tn(a, b):
    return lax.dot_general(a, b, (((0,), (0,)), ((), ())), preferred_element_type=F32)


def _dot_chunked(a, w_ref):
    return jnp.concatenate([_dot(a, w_ref[c]) for c in range(w_ref.shape[0])], axis=1)


def _layer_norm(z, g, b):
    mu = jnp.mean(z, axis=-1, keepdims=True)
    zc = z - mu
    var = jnp.mean(zc * zc, axis=-1, keepdims=True)
    return zc * lax.rsqrt(var + LN_EPS) * g + b


def _split3(a):
    hi = a.astype(BF16)
    r1 = a - hi.astype(F32)
    mid = r1.astype(BF16)
    lo = (r1 - mid.astype(F32)).astype(BF16)
    return hi, mid, lo


def _gelu(x):
    return 0.5 * x * (1.0 + lax.erf(x * (2.0 ** -0.5)))


def _log_sigmoid(x):
    return jnp.minimum(x, 0.0) - jnp.log1p(jnp.exp(-jnp.abs(x)))


def _resident(shape, index):
    return pl.BlockSpec(shape, lambda t: index, pipeline_mode=pl.Buffered(1))


def _cast_plan(weights, steps):
    in_specs, out_specs, out_shapes, args = [], [], [], []
    for kind, w, layer in weights:
        args.append(w)
        if kind == "transposed":
            _, N, K = w.shape
            last = N // WCHUNK - 1
            assert steps > last
            in_specs.append(pl.BlockSpec((None, WCHUNK, K),
                                         lambda t, layer=layer, last=last: (layer, jnp.minimum(t, last), 0)))
            out_specs.append(pl.BlockSpec((None, K, WCHUNK), lambda t, last=last: (jnp.minimum(t, last), 0, 0)))
            out_shapes.append(jax.ShapeDtypeStruct((last + 1, K, WCHUNK), BF16))
            continue
        _, K, N = w.shape
        hold = 1
        while K % (steps // hold) or (K // (steps // hold)) % V7X_BF16_ROWS:
            hold *= 2
        rows = K // (steps // hold)
        in_specs.append(pl.BlockSpec((None, rows, N), lambda t, layer=layer, hold=hold: (layer, t // hold, 0)))
        if kind == "chunked":
            n = N // WCHUNK
            out_specs.append(pl.BlockSpec((n, rows, WCHUNK), lambda t, hold=hold: (0, t // hold, 0)))
            out_shapes.append(jax.ShapeDtypeStruct((n, K, WCHUNK), BF16))
        else:
            out_specs.append(pl.BlockSpec((rows, N), lambda t, hold=hold: (t // hold, 0)))
            out_shapes.append(jax.ShapeDtypeStruct((K, N), BF16))
    return in_specs, out_specs, out_shapes, args


def _split_refs(refs, n_in, n_cast):
    a, b, c = n_in, n_in + n_cast, n_in + 2 * n_cast + 1
    return refs[:a], refs[a:b], refs[b], refs[b + 1:c], refs[c:]


def _run_casts(cast_in, cast_out):
    for src, dst in zip(cast_in, cast_out):
        if len(dst.shape) == 3:
            width = dst.shape[2]
            for c in range(dst.shape[0]):
                dst[c] = src[:, c * width:(c + 1) * width].astype(BF16)
        elif dst.shape != src.shape:
            dst[...] = src[...].T.astype(BF16)
        else:
            dst[...] = src[...].astype(BF16)


def _call(kernel_fn, name, x, row_spec, main_specs, main_args, cast_weights, scratch_shapes):
    S, D = x.shape
    steps = S // row_spec.block_shape[0]
    c_in, c_out, c_shapes, c_args = _cast_plan(cast_weights, steps)
    outs = pl.pallas_call(
        functools.partial(kernel_fn, 1 + len(main_args), len(c_args)),
        grid=(steps,),
        in_specs=[row_spec] + main_specs + c_in,
        out_specs=[row_spec] + c_out,
        out_shape=[jax.ShapeDtypeStruct((S, D), F32)] + c_shapes,
        scratch_shapes=scratch_shapes,
        compiler_params=pltpu.CompilerParams(
            dimension_semantics=("arbitrary",), vmem_limit_bytes=VMEM_LIMIT_BYTES),
        name=name,
    )(x, *main_args, *c_args)
    return outs[0], outs[1:]


def _ffn_kernel(with_ple, ln_row, n_in, n_cast, *refs):
    ins, cast_in, o_ref, cast_out, (a_ref,) = _split_refs(refs, n_in, n_cast)
    if with_ple:
        x_ref, wgu_ref, wd_ref, lng_ref, lnb_ref, p_ref, wpg_ref, bpg_ref, wpp_ref = ins
    else:
        x_ref, wgu_ref, wd_ref, lng_ref, lnb_ref = ins
    _run_casts(cast_in, cast_out)
    xb = x_ref[...].astype(BF16)
    for c in range(FFN_HIDDEN // FFN_COLS):
        lo = c * FFN_COLS
        g = _dot(xb, wgu_ref[:, lo:lo + FFN_COLS])
        u = _dot(xb, wgu_ref[:, FFN_HIDDEN + lo:FFN_HIDDEN + lo + FFN_COLS])
        a_ref[:, lo:lo + FFN_COLS] = (jax.nn.silu(g) * u).astype(BF16)
    n_sub = a_ref.shape[0] // FFN_SUB

    def down(r):
        rs = slice(r * FFN_SUB, (r + 1) * FFN_SUB)
        y = _dot_chunked(a_ref[rs, :], wd_ref)
        return _layer_norm(DN_ALPHA * x_ref[rs, :] + 0.5 * y,
                           lng_ref[ln_row:ln_row + 1, :], lnb_ref[ln_row:ln_row + 1, :])

    x1 = down(0)
    for r in range(n_sub):
        rs = slice(r * FFN_SUB, (r + 1) * FFN_SUB)
        x1_next = down(r + 1) if r + 1 < n_sub else None
        if with_ple:
            gate = jax.nn.sigmoid(_dot_chunked(x1.astype(BF16), wpg_ref) + bpg_ref[...])
            emb = _dot(p_ref[rs, :].astype(BF16), wpp_ref[...])
            x1 = _layer_norm(DN_ALPHA * x1 + gate * emb, lng_ref[3:4, :], lnb_ref[3:4, :])
        o_ref[rs, :] = x1
        x1 = x1_next


def _ffn_call(x, wgu, wd, ln_g, ln_b, layer, ln_row, cast_weights, ple=None):
    S, D = x.shape
    F = FFN_HIDDEN
    tm = FFN_ROWS if ple is None else FFN_PLE_ROWS
    main_specs = [
        _resident((D, 2 * F), (0, 0)),
        _resident((D // WCHUNK, F, WCHUNK), (0, 0, 0)),
        _resident((None, 4, D), (layer, 0, 0)),
        _resident((None, 4, D), (layer, 0, 0)),
    ]
    main_args = [wgu, wd, ln_g, ln_b]
    if ple is not None:
        p, wpg, bpg, wpp = ple
        main_specs += [
            pl.BlockSpec((None, tm, PLE_DIM), lambda t: (layer, t, 0)),
            _resident((D // WCHUNK, D, WCHUNK), (0, 0, 0)),
            _resident((None, 1, D), (layer, 0, 0)),
            _resident((PLE_DIM, D), (0, 0)),
        ]
        main_args += [p, wpg, bpg, wpp]
    return _call(functools.partial(_ffn_kernel, ple is not None, ln_row),
                 "ffn_ple" if ple is not None else "ffn",
                 x, pl.BlockSpec((tm, D), lambda t: (t, 0)), main_specs, main_args, cast_weights,
                 [pltpu.VMEM((tm, F), BF16)])


def _conv_silu(qk_ref, act_ref, conv_ref, row0, col0, scale):
    L = ML_ROWS
    HALO = V7X_SUBLANES
    groups = L // V7X_SUBLANES
    outs = []
    for b in range(ML_DQK // V7X_LANES):
        blk = col0 // V7X_LANES + b
        acc = None
        for j in range(ML_CONV):
            start = row0 + HALO - (ML_CONV - 1) + j
            tap = jnp.concatenate(
                [qk_ref[blk, pl.ds(start + s, groups, stride=V7X_SUBLANES), :] for s in range(V7X_SUBLANES)], axis=0)
            term = conv_ref[j:j + 1, blk * V7X_LANES:(blk + 1) * V7X_LANES] * tap
            acc = term if acc is None else acc + term
        act = jax.nn.silu(acc) * scale if scale != 1.0 else jax.nn.silu(acc)
        for s in range(V7X_SUBLANES):
            act_ref[blk, pl.ds(row0 + s, groups, stride=V7X_SUBLANES), :] = act[s * groups:(s + 1) * groups, :]
        outs.append(act_ref[blk, row0:row0 + L, :])
    return jnp.concatenate(outs, axis=1)


def _mlstm_kernel(n_in, n_cast, *refs):
    ins, cast_in, o_ref, cast_out, scratch = _split_refs(refs, n_in, n_cast)
    (x_ref, win_ref, bin_ref, wgc_ref, wgr_ref, bgc_ref, bgr_ref, conv_ref, ng_ref,
     wout_ref, lng_ref, lnb_ref) = ins
    qk_ref, act_ref, ct_ref, n_ref, m_ref, hb_ref = scratch
    L = ML_ROWS
    HALO = V7X_SUBLANES
    n_chunks = x_ref.shape[0] // L
    n_blk = WCHUNK // V7X_LANES

    @pl.when(pl.program_id(0) == 0)
    def _init():
        qk_ref[:, 0:HALO, :] = jnp.zeros((2 * ML_QK // V7X_LANES, HALO, V7X_LANES), F32)
        ct_ref[...] = jnp.zeros_like(ct_ref)
        n_ref[...] = jnp.zeros_like(n_ref)
        m_ref[...] = jnp.zeros_like(m_ref)

    _run_casts(cast_in, cast_out)
    xbs = [x_ref[c * L:(c + 1) * L, :].astype(BF16) for c in range(n_chunks)]

    ri = lax.broadcasted_iota(jnp.int32, (L, L), 0)
    ci = lax.broadcasted_iota(jnp.int32, (L, L), 1)
    causal = ri >= ci
    lower = causal.astype(BF16)
    upper = (ri <= ci).astype(BF16)

    def qk_piece(c, piece):
        pre = _dot(xbs[c], win_ref[piece]) + bin_ref[:, piece * WCHUNK:(piece + 1) * WCHUNK]
        for b in range(n_blk):
            qk_ref[piece * n_blk + b, HALO + c * L:HALO + (c + 1) * L, :] = pre[:, b * V7X_LANES:(b + 1) * V7X_LANES]

    def gates(c):
        g_col = _dot(xbs[c], wgc_ref[...]) + bgc_ref[...]
        g_row = _dot_nt(wgr_ref[...], xbs[c]) + bgr_ref[...]
        fh, fm, fl = _split3(_log_sigmoid(g_col))
        bcum_col = _dot(lower, fh) + _dot(lower, fm) + _dot(lower, fl)
        fh, fm, fl = _split3(_log_sigmoid(g_row))
        bcum_row = _dot(fh, upper) + _dot(fm, upper) + _dot(fl, upper)
        return g_col, g_row, bcum_col, bcum_row

    def head(c, h, gate_vals):
        g_col, g_row, bcum_col, bcum_row = gate_vals
        xb = xbs[c]
        q = _conv_silu(qk_ref, act_ref, conv_ref, c * L, h * ML_DQK, 1.0)
        k = _conv_silu(qk_ref, act_ref, conv_ref, c * L, ML_QK + h * ML_DQK, ML_DQK ** -0.5)
        qb = q.astype(BF16)
        kb = k.astype(BF16)
        vlo = 2 * ML_QK + h * ML_DV
        v = _dot(xb, win_ref[vlo // WCHUNK]) + bin_ref[:, vlo:vlo + ML_DV]
        vb = v.astype(BF16)

        i_col = g_col[:, h:h + 1]
        b_col = bcum_col[:, ML_HEADS + h:ML_HEADS + h + 1]
        i_row = g_row[h:h + 1, :]
        b_row = bcum_row[ML_HEADS + h:ML_HEADS + h + 1, :]
        m_prev = m_ref[h][:, 0:1]
        ct = ct_ref[h]
        n_prev = n_ref[h]

        d_log = jnp.where(causal, b_col - b_row + i_row, -jnp.inf)
        inter = b_col + m_prev
        m_t = jnp.maximum(inter, jnp.max(d_log, axis=-1, keepdims=True))
        s_mat = _dot_nt(qb, kb) * jnp.exp(d_log - m_t)
        w_inter = jnp.exp(inter - m_t)
        num = _dot(s_mat.astype(BF16), vb) + w_inter * _dot(qb, ct.astype(BF16))
        den = jnp.sum(s_mat, axis=-1, keepdims=True) + w_inter * jnp.sum(q * n_prev, axis=-1, keepdims=True)
        hh = num / jnp.maximum(jnp.abs(den), jnp.exp(-m_t))

        b_last = b_col[L - 1:L, :]
        w_log = b_last - b_col + i_col
        m_new = jnp.maximum(b_last + m_prev, jnp.max(w_log, axis=0, keepdims=True))
        decay = jnp.exp(b_last + m_prev - m_new)
        kw = k * jnp.exp(w_log - m_new)
        ct_ref[h] = decay * ct + _dot_tn(kw.astype(BF16), vb)
        n_ref[h] = decay * n_prev + jnp.sum(kw, axis=0, keepdims=True)
        m_ref[h] = jnp.broadcast_to(m_new, m_ref.shape[1:])

        mu = jnp.mean(hh, axis=-1, keepdims=True)
        hc = hh - mu
        var = jnp.mean(hc * hc, axis=-1, keepdims=True)
        hn = hc * lax.rsqrt(var + LN_EPS) * ng_ref[:, h * ML_DV:(h + 1) * ML_DV]
        zlo = 2 * ML_QK + ML_INNER + h * ML_DV
        z = _dot(xb, win_ref[zlo // WCHUNK]) + bin_ref[:, zlo:zlo + ML_DV]
        hb_ref[c * L:(c + 1) * L, h * ML_DV:(h + 1) * ML_DV] = (hn * jax.nn.sigmoid(z)).astype(BF16)

    def out_half(c, half):
        return _dot(hb_ref[c * L:(c + 1) * L, :], wout_ref[half])

    def finish(c, halves):
        rows = slice(c * L, (c + 1) * L)
        y = jnp.concatenate(halves, axis=1)
        o_ref[rows, :] = _layer_norm(DN_ALPHA * x_ref[rows, :] + y, lng_ref[1:2, :], lnb_ref[1:2, :])

    n_pieces = 2 * ML_QK // WCHUNK
    for piece in range(n_pieces):
        qk_piece(0, piece)
    gate_vals = [gates(c) for c in range(n_chunks)]
    for c in range(n_chunks):
        halves = []
        for h in range(ML_HEADS):
            head(c, h, gate_vals[c])
            if c + 1 < n_chunks:
                qk_piece(c + 1, h)
            if c > 0 and h < 2:
                halves.append(out_half(c - 1, h))
                if h == 1:
                    finish(c - 1, halves)
    finish(n_chunks - 1, [out_half(n_chunks - 1, 0), out_half(n_chunks - 1, 1)])

    qk_ref[:, 0:HALO, :] = qk_ref[:, n_chunks * L:n_chunks * L + HALO, :]


def _mlstm_call(x, w_in, b_in, wg_col, wg_row, bg_col, bg_row, conv_w, norm_g, w_out, ln_g, ln_b, layer, j,
                cast_weights):
    S, D = x.shape
    L = ML_ROWS
    main_specs = [
        _resident((ML_MAIN // WCHUNK, D, WCHUNK), (0, 0, 0)),
        _resident((None, 1, ML_MAIN), (j, 0, 0)),
        _resident((None, D, 2 * ML_HEADS), (j, 0, 0)),
        _resident((None, 4 * ML_HEADS, D), (j, 0, 0)),
        _resident((None, 1, 2 * ML_HEADS), (j, 0, 0)),
        _resident((None, 4 * ML_HEADS, 1), (j, 0, 0)),
        _resident((None, ML_CONV, 2 * ML_QK), (j, 0, 0)),
        _resident((None, 1, ML_INNER), (j, 0, 0)),
        _resident((D // WCHUNK, ML_INNER, WCHUNK), (0, 0, 0)),
        _resident((None, 4, D), (layer, 0, 0)),
        _resident((None, 4, D), (layer, 0, 0)),
    ]
    main_args = [w_in, b_in, wg_col, wg_row, bg_col, bg_row, conv_w, norm_g, w_out, ln_g, ln_b]
    scratch = [
        pltpu.VMEM((2 * ML_QK // V7X_LANES, V7X_SUBLANES + ML_TILE, V7X_LANES), F32),
        pltpu.VMEM((2 * ML_QK // V7X_LANES, ML_TILE, V7X_LANES), F32),
        pltpu.VMEM((ML_HEADS, ML_DQK, ML_DV), F32),
        pltpu.VMEM((ML_HEADS, 1, ML_DQK), F32),
        pltpu.VMEM((ML_HEADS, 1, 128), F32),
        pltpu.VMEM((ML_TILE, ML_INNER), BF16),
    ]
    return _call(_mlstm_kernel, "mlstm", x, pl.BlockSpec((ML_TILE, D), lambda t: (t, 0)),
                 main_specs, main_args, cast_weights, scratch)


def _group_cols(w_ref, col0):
    off = col0 % WCHUNK
    return w_ref[col0 // WCHUNK, :, off:off + GM_GDIM]


def _gmlp_kernel(n_in, n_cast, *refs):
    ins, cast_in, o_ref, cast_out, (v_ref, u_ref, y_ref) = _split_refs(refs, n_in, n_cast)
    x_ref, win_ref, bin_ref, vng_ref, vnb_ref, ws_ref, bst_ref, wout_ref, lng_ref, lnb_ref = ins
    _run_casts(cast_in, cast_out)
    xb = x_ref[...].astype(BF16)
    for g in range(GM_GROUPS):
        lo = g * GM_GDIM
        vlo = GM_WIDTH + lo
        v_ref[:, lo:lo + GM_GDIM] = _gelu(_dot(xb, _group_cols(win_ref, vlo)) + bin_ref[:, vlo:vlo + GM_GDIM])
    for g in range(GM_GROUPS):
        lo = g * GM_GDIM
        u_ref[:, lo:lo + GM_GDIM] = _gelu(_dot(xb, _group_cols(win_ref, lo)) + bin_ref[:, lo:lo + GM_GDIM])
    v = v_ref[...]
    mu = jnp.mean(v, axis=-1, keepdims=True)
    vc = v - mu
    rstd = lax.rsqrt(jnp.mean(vc * vc, axis=-1, keepdims=True) + LN_EPS)

    blk_r = lax.broadcasted_iota(jnp.int32, (GM_BLOCK, GM_BLOCK), 0) // CHUNK
    blk_c = lax.broadcasted_iota(jnp.int32, (GM_BLOCK, GM_BLOCK), 1) // CHUNK
    chunk_causal = blk_r >= blk_c
    for g in range(GM_GROUPS):
        lo = g * GM_GDIM
        vn = ((v_ref[:, lo:lo + GM_GDIM] - mu) * rstd * vng_ref[:, lo:lo + GM_GDIM]
              + vnb_ref[:, lo:lo + GM_GDIM]).astype(BF16)
        wsm = jnp.where(chunk_causal, ws_ref[g], 0.0).astype(BF16)
        bias = bst_ref[:, g:g + 1]
        for nb in range(GM_ROWS // GM_BLOCK):
            r0 = nb * GM_BLOCK
            s = _dot(wsm, vn[r0:r0 + GM_BLOCK, :]) + bias
            y_ref[r0:r0 + GM_BLOCK, lo:lo + GM_GDIM] = (u_ref[r0:r0 + GM_BLOCK, lo:lo + GM_GDIM] * s).astype(BF16)
    for r in range(GM_ROWS // GM_SUB):
        rs = slice(r * GM_SUB, (r + 1) * GM_SUB)
        y = _dot_chunked(y_ref[rs, :], wout_ref)
        o_ref[rs, :] = _layer_norm(DN_ALPHA * x_ref[rs, :] + y, lng_ref[1:2, :], lnb_ref[1:2, :])


def _gmlp_call(x, w_in, b_in, vn_g, vn_b, ws, bs_t, w_out, ln_g, ln_b, layer, j, cast_weights):
    S, D = x.shape
    main_specs = [
        _resident((2 * GM_WIDTH // WCHUNK, D, WCHUNK), (0, 0, 0)),
        _resident((None, 1, 2 * GM_WIDTH), (j, 0, 0)),
        _resident((None, 1, GM_WIDTH), (j, 0, 0)),
        _resident((None, 1, GM_WIDTH), (j, 0, 0)),
        _resident((None, GM_GROUPS, GM_BLOCK, GM_BLOCK), (j, 0, 0, 0)),
        _resident((None, GM_BLOCK, GM_GROUPS), (j, 0, 0)),
        _resident((D // WCHUNK, GM_WIDTH, WCHUNK), (0, 0, 0)),
        _resident((None, 4, D), (layer, 0, 0)),
        _resident((None, 4, D), (layer, 0, 0)),
    ]
    main_args = [w_in, b_in, vn_g, vn_b, ws, bs_t, w_out, ln_g, ln_b]
    scratch = [
        pltpu.VMEM((GM_ROWS, GM_WIDTH), F32),
        pltpu.VMEM((GM_ROWS, GM_WIDTH), F32),
        pltpu.VMEM((GM_ROWS, GM_WIDTH), BF16),
    ]
    return _call(_gmlp_kernel, "gmlp", x, pl.BlockSpec((GM_ROWS, D), lambda t: (t, 0)),
                 main_specs, main_args, cast_weights, scratch)


def kernel(x, p, ln_g, ln_b, ffn1_wgu, ffn1_wd, ffn2_wgu, ffn2_wd, ml_w_in, ml_b_in, ml_conv, ml_norm_g, ml_w_out, gm_w_in, gm_b_in, gm_vn_g, gm_vn_b, gm_ws, gm_bs, gm_w_out, ple_wp, ple_wg, ple_bg):
    B, S, D = x.shape
    assert (B, S, D) == (1, SEQ, D_MODEL)
    xs = x.reshape(S, D)
    ps = p.reshape(DEPTH, S, PLE_DIM)

    ple_bg_r = ple_bg[:, None, :]
    ml_b_main = ml_b_in[:, None, :ML_MAIN]
    ml_wg_col = ml_w_in[:, :, ML_MAIN:].astype(BF16)
    ml_wg_row = jnp.pad(jnp.swapaxes(ml_w_in[:, :, ML_MAIN:], 1, 2),
                        ((0, 0), (0, 2 * ML_HEADS), (0, 0))).astype(BF16)
    ml_bg_col = ml_b_in[:, None, ML_MAIN:]
    ml_bg_row = jnp.pad(ml_b_in[:, ML_MAIN:], ((0, 0), (0, 2 * ML_HEADS)))[:, :, None]
    ml_norm_g_r = ml_norm_g[:, None, :]
    ml_w_in_t = jnp.swapaxes(ml_w_in, 1, 2)
    gm_b_in_r = gm_b_in[:, None, :]
    gm_vn_g_r, gm_vn_b_r = gm_vn_g[:, None, :], gm_vn_b[:, None, :]
    gm_bs_t = jnp.swapaxes(gm_bs, 1, 2)

    wd0 = ffn1_wd[0].astype(BF16)
    ffn1_w = (ffn1_wgu[0].astype(BF16),
              jnp.stack([wd0[:, c * WCHUNK:(c + 1) * WCHUNK] for c in range(D_MODEL // WCHUNK)]))
    for i in range(DEPTH):
        j = i // 2
        if i % 2 == 0:
            mixer_f32 = [("transposed", ml_w_in_t, j), ("chunked", ml_w_out, j)]
        else:
            mixer_f32 = [("chunked", gm_w_in, j), ("chunked", gm_w_out, j)]
        ffn2_rest_f32 = [("chunked", ffn2_wd, i), ("chunked", ple_wg, i), ("plain", ple_wp, i)]
        xs, cast_w = _ffn_call(xs, *ffn1_w, ln_g, ln_b, i, 0, mixer_f32 + ffn2_rest_f32)
        mixer_w, ffn2_rest_w = cast_w[:2], cast_w[2:]
        ffn2_f32 = [("plain", ffn2_wgu, i)]
        if i % 2 == 0:
            xs, ffn2_w = _mlstm_call(xs, mixer_w[0], ml_b_main, ml_wg_col, ml_wg_row, ml_bg_col, ml_bg_row,
                                     ml_conv, ml_norm_g_r, mixer_w[1], ln_g, ln_b, i, j, ffn2_f32)
        else:
            xs, ffn2_w = _gmlp_call(xs, mixer_w[0], gm_b_in_r, gm_vn_g_r, gm_vn_b_r, gm_ws, gm_bs_t,
                                    mixer_w[1], ln_g, ln_b, i, j, ffn2_f32)
        next_f32 = [("plain", ffn1_wgu, i + 1), ("chunked", ffn1_wd, i + 1)] if i + 1 < DEPTH else []
        xs, ffn1_w = _ffn_call(xs, ffn2_w[0], ffn2_rest_w[0], ln_g, ln_b, i, 2, next_f32,
                               ple=(ps, ffn2_rest_w[1], ple_bg_r, ffn2_rest_w[2]))
    return xs.reshape(B, S, D)
```

```python
import functools

import jax
import jax.numpy as jnp
from jax import lax
from jax.experimental import pallas as pl
from jax.experimental.pallas import tpu as pltpu

D_MODEL = 1024
SEQ = 16384
DEPTH = 4
CHUNK = 64
PLE_DIM = 256
FFN_HIDDEN = 2816
ML_HEADS = 4
ML_DQK = 256
ML_DV = 512
ML_INNER = ML_HEADS * ML_DV
ML_QK = ML_HEADS * ML_DQK
ML_MAIN = 2 * ML_QK + 2 * ML_INNER
ML_CONV = 4
GM_BLOCK = 128
GM_GROUPS = 8
GM_WIDTH = 2 * D_MODEL
GM_GDIM = GM_WIDTH // GM_GROUPS
DN_ALPHA = (2.0 * DEPTH) ** 0.25
LN_EPS = 1e-5

V7X_MXU_DIM = 256
V7X_SUBLANES = 8
V7X_LANES = 128
V7X_BF16_ROWS = 16
V7X_VMEM_BYTES = 64 * 1024 * 1024
VMEM_LIMIT_BYTES = V7X_VMEM_BYTES * 7 // 8

FFN_ROWS = 1024
FFN_PLE_ROWS = 1024
FFN_COLS = V7X_MXU_DIM
FFN_SUB = 256
ML_ROWS = 256
ML_TILE = 512
GM_ROWS = 1024
GM_SUB = 256
WCHUNK = 512

F32 = jnp.float32
BF16 = jnp.bfloat16


def _dot(a, b):
    return jnp.dot(a, b, preferred_element_type=F32)


def _dot_nt(a, b):
    return lax.dot_general(a, b, (((1,), (1,)), ((), ())), preferred_element_type=F32)


def _dot_tn(a, b):
    return lax.dot_general(a, b, (((0,), (0,)), ((), ())), preferred_element_type=F32)


def _dot_chunked(a, w_ref):
    return jnp.concatenate([_dot(a, w_ref[c]) for c in range(w_ref.shape[0])], axis=1)


def _layer_norm(z, g, b):
    mu = jnp.mean(z, axis=-1, keepdims=True)
    zc = z - mu
    var = jnp.mean(zc * zc, axis=-1, keepdims=True)
    return zc * lax.rsqrt(var + LN_EPS) * g + b


def _split3(a):
    hi = a.astype(BF16)
    r1 = a - hi.astype(F32)
    mid = r1.astype(BF16)
    lo = (r1 - mid.astype(F32)).astype(BF16)
    return hi, mid, lo


def _gelu(x):
    return 0.5 * x * (1.0 + lax.erf(x * (2.0 ** -0.5)))


def _log_sigmoid(x):
    return jnp.minimum(x, 0.0) - jnp.log1p(jnp.exp(-jnp.abs(x)))


def _resident(shape, index):
    return pl.BlockSpec(shape, lambda t: index, pipeline_mode=pl.Buffered(1))


def _cast_plan(weights, steps):
    in_specs, out_specs, out_shapes, args = [], [], [], []
    for kind, w, layer in weights:
        args.append(w)
        if kind == "transposed":
            _, N, K = w.shape
            last = N // WCHUNK - 1
            assert steps > last
            in_specs.append(pl.BlockSpec((None, WCHUNK, K),
                                         lambda t, layer=layer, last=last: (layer, jnp.minimum(t, last), 0)))
            out_specs.append(pl.BlockSpec((None, K, WCHUNK), lambda t, last=last: (jnp.minimum(t, last), 0, 0)))
            out_shapes.append(jax.ShapeDtypeStruct((last + 1, K, WCHUNK), BF16))
            continue
        _, K, N = w.shape
        hold = 1
        while K % (steps // hold) or (K // (steps // hold)) % V7X_BF16_ROWS:
            hold *= 2
        rows = K // (steps // hold)
        in_specs.append(pl.BlockSpec((None, rows, N), lambda t, layer=layer, hold=hold: (layer, t // hold, 0)))
        if kind == "chunked":
            n = N // WCHUNK
            out_specs.append(pl.BlockSpec((n, rows, WCHUNK), lambda t, hold=hold: (0, t // hold, 0)))
            out_shapes.append(jax.ShapeDtypeStruct((n, K, WCHUNK), BF16))
        else:
            out_specs.append(pl.BlockSpec((rows, N), lambda t, hold=hold: (t // hold, 0)))
            out_shapes.append(jax.ShapeDtypeStruct((K, N), BF16))
    return in_specs, out_specs, out_shapes, args


def _split_refs(refs, n_in, n_cast):
    a, b, c = n_in, n_in + n_cast, n_in + 2 * n_cast + 1
    return refs[:a], refs[a:b], refs[b], refs[b + 1:c], refs[c:]


def _run_casts(cast_in, cast_out):
    for src, dst in zip(cast_in, cast_out):
        if len(dst.shape) == 3:
            width = dst.shape[2]
            for c in range(dst.shape[0]):
                dst[c] = src[:, c * width:(c + 1) * width].astype(BF16)
        elif dst.shape != src.shape:
            dst[...] = src[...].T.astype(BF16)
        else:
            dst[...] = src[...].astype(BF16)


def _call(kernel_fn, name, x, row_spec, main_specs, main_args, cast_weights, scratch_shapes):
    S, D = x.shape
    steps = S // row_spec.block_shape[0]
    c_in, c_out, c_shapes, c_args = _cast_plan(cast_weights, steps)
    outs = pl.pallas_call(
        functools.partial(kernel_fn, 1 + len(main_args), len(c_args)),
        grid=(steps,),
        in_specs=[row_spec] + main_specs + c_in,
        out_specs=[row_spec] + c_out,
        out_shape=[jax.ShapeDtypeStruct((S, D), F32)] + c_shapes,
        scratch_shapes=scratch_shapes,
        compiler_params=pltpu.CompilerParams(
            dimension_semantics=("arbitrary",), vmem_limit_bytes=VMEM_LIMIT_BYTES),
        name=name,
    )(x, *main_args, *c_args)
    return outs[0], outs[1:]


def _ffn_kernel(with_ple, ln_row, n_in, n_cast, *refs):
    ins, cast_in, o_ref, cast_out, (a_ref,) = _split_refs(refs, n_in, n_cast)
    if with_ple:
        x_ref, wgu_ref, wd_ref, lng_ref, lnb_ref, p_ref, wpg_ref, bpg_ref, wpp_ref = ins
    else:
        x_ref, wgu_ref, wd_ref, lng_ref, lnb_ref = ins
    _run_casts(cast_in, cast_out)
    xb = x_ref[...].astype(BF16)
    for c in range(FFN_HIDDEN // FFN_COLS):
        lo = c * FFN_COLS
        g = _dot(xb, wgu_ref[:, lo:lo + FFN_COLS])
        u = _dot(xb, wgu_ref[:, FFN_HIDDEN + lo:FFN_HIDDEN + lo + FFN_COLS])
        a_ref[:, lo:lo + FFN_COLS] = (jax.nn.silu(g) * u).astype(BF16)
    n_sub = a_ref.shape[0] // FFN_SUB

    def down(r):
        rs = slice(r * FFN_SUB, (r + 1) * FFN_SUB)
        y = _dot_chunked(a_ref[rs, :], wd_ref)
        return _layer_norm(DN_ALPHA * x_ref[rs, :] + 0.5 * y,
                           lng_ref[ln_row:ln_row + 1, :], lnb_ref[ln_row:ln_row + 1, :])

    x1 = down(0)
    for r in range(n_sub):
        rs = slice(r * FFN_SUB, (r + 1) * FFN_SUB)
        x1_next = down(r + 1) if r + 1 < n_sub else None
        if with_ple:
            gate = jax.nn.sigmoid(_dot_chunked(x1.astype(BF16), wpg_ref) + bpg_ref[...])
            emb = _dot(p_ref[rs, :].astype(BF16), wpp_ref[...])
            x1 = _layer_norm(DN_ALPHA * x1 + gate * emb, lng_ref[3:4, :], lnb_ref[3:4, :])
        o_ref[rs, :] = x1
        x1 = x1_next


def _ffn_call(x, wgu, wd, ln_g, ln_b, layer, ln_row, cast_weights, ple=None):
    S, D = x.shape
    F = FFN_HIDDEN
    tm = FFN_ROWS if ple is None else FFN_PLE_ROWS
    main_specs = [
        _resident((D, 2 * F), (0, 0)),
        _resident((D // WCHUNK, F, WCHUNK), (0, 0, 0)),
        _resident((None, 4, D), (layer, 0, 0)),
        _resident((None, 4, D), (layer, 0, 0)),
    ]
    main_args = [wgu, wd, ln_g, ln_b]
    if ple is not None:
        p, wpg, bpg, wpp = ple
        main_specs += [
            pl.BlockSpec((None, tm, PLE_DIM), lambda t: (layer, t, 0)),
            _resident((D // WCHUNK, D, WCHUNK), (0, 0, 0)),
            _resident((None, 1, D), (layer, 0, 0)),
            _resident((PLE_DIM, D), (0, 0)),
        ]
        main_args += [p, wpg, bpg, wpp]
    return _call(functools.partial(_ffn_kernel, ple is not None, ln_row),
                 "ffn_ple" if ple is not None else "ffn",
                 x, pl.BlockSpec((tm, D), lambda t: (t, 0)), main_specs, main_args, cast_weights,
                 [pltpu.VMEM((tm, F), BF16)])


def _conv_silu(qk_ref, act_ref, conv_ref, row0, col0, scale):
    L = ML_ROWS
    HALO = V7X_SUBLANES
    groups = L // V7X_SUBLANES
    outs = []
    for b in range(ML_DQK // V7X_LANES):
        blk = col0 // V7X_LANES + b
        acc = None
        for j in range(ML_CONV):
            start = row0 + HALO - (ML_CONV - 1) + j
            tap = jnp.concatenate(
                [qk_ref[blk, pl.ds(start + s, groups, stride=V7X_SUBLANES), :] for s in range(V7X_SUBLANES)], axis=0)
            term = conv_ref[j:j + 1, blk * V7X_LANES:(blk + 1) * V7X_LANES] * tap
            acc = term if acc is None else acc + term
        act = jax.nn.silu(acc) * scale if scale != 1.0 else jax.nn.silu(acc)
        for s in range(V7X_SUBLANES):
            act_ref[blk, pl.ds(row0 + s, groups, stride=V7X_SUBLANES), :] = act[s * groups:(s + 1) * groups, :]
        outs.append(act_ref[blk, row0:row0 + L, :])
    return jnp.concatenate(outs, axis=1)


def _mlstm_kernel(n_in, n_cast, *refs):
    ins, cast_in, o_ref, cast_out, scratch = _split_refs(refs, n_in, n_cast)
    (x_ref, win_ref, bin_ref, wgr_ref, bgr_ref, conv_ref, ng_ref,
     wout_ref, lng_ref, lnb_ref) = ins
    qk_ref, act_ref, ct_ref, n_ref, m_ref, hb_ref = scratch
    L = ML_ROWS
    HALO = V7X_SUBLANES
    n_chunks = x_ref.shape[0] // L
    n_blk = WCHUNK // V7X_LANES

    @pl.when(pl.program_id(0) == 0)
    def _init():
        qk_ref[:, 0:HALO, :] = jnp.zeros((2 * ML_QK // V7X_LANES, HALO, V7X_LANES), F32)
        ct_ref[...] = jnp.zeros_like(ct_ref)
        n_ref[...] = jnp.zeros_like(n_ref)
        m_ref[...] = jnp.zeros_like(m_ref)

    _run_casts(cast_in, cast_out)
    xbs = [x_ref[c * L:(c + 1) * L, :].astype(BF16) for c in range(n_chunks)]

    ri = lax.broadcasted_iota(jnp.int32, (L, L), 0)
    ci = lax.broadcasted_iota(jnp.int32, (L, L), 1)
    causal = ri >= ci
    upper = (ri <= ci).astype(BF16)

    def qk_piece(c, piece):
        pre = _dot(xbs[c], win_ref[piece]) + bin_ref[:, piece * WCHUNK:(piece + 1) * WCHUNK]
        for b in range(n_blk):
            qk_ref[piece * n_blk + b, HALO + c * L:HALO + (c + 1) * L, :] = pre[:, b * V7X_LANES:(b + 1) * V7X_LANES]

    def gates(c):
        g_row = _dot_nt(wgr_ref[...], xbs[c]) + bgr_ref[...]
        fh, fm, fl = _split3(_log_sigmoid(g_row))
        bcum_row = _dot(fh, upper) + _dot(fm, upper) + _dot(fl, upper)
        n_rows = g_row.shape[0]
        pad = jnp.zeros((V7X_LANES - 2 * n_rows, L), F32)
        cols = jnp.concatenate([g_row, bcum_row, pad], axis=0).T
        return cols[:, 0:n_rows], g_row, cols[:, n_rows:2 * n_rows], bcum_row

    def head(c, h, gate_vals):
        g_col, g_row, bcum_col, bcum_row = gate_vals
        xb = xbs[c]
        q = _conv_silu(qk_ref, act_ref, conv_ref, c * L, h * ML_DQK, 1.0)
        k = _conv_silu(qk_ref, act_ref, conv_ref, c * L, ML_QK + h * ML_DQK, ML_DQK ** -0.5)
        qb = q.astype(BF16)
        kb = k.astype(BF16)
        vlo = 2 * ML_QK + h * ML_DV
        v = _dot(xb, win_ref[vlo // WCHUNK]) + bin_ref[:, vlo:vlo + ML_DV]
        vb = v.astype(BF16)

        i_col = g_col[:, h:h + 1]
        b_col = bcum_col[:, ML_HEADS + h:ML_HEADS + h + 1]
        i_row = g_row[h:h + 1, :]
        b_row = bcum_row[ML_HEADS + h:ML_HEADS + h + 1, :]
        m_prev = m_ref[h][:, 0:1]
        ct = ct_ref[h]
        n_prev = n_ref[h]

        d_log = jnp.where(causal, b_col - b_row + i_row, -jnp.inf)
        inter = b_col + m_prev
        m_t = jnp.maximum(inter, jnp.max(d_log, axis=-1, keepdims=True))
        s_mat = _dot_nt(qb, kb) * jnp.exp(d_log - m_t)
        w_inter = jnp.exp(inter - m_t)
        num = _dot(s_mat.astype(BF16), vb) + w_inter * _dot(qb, ct.astype(BF16))
        den = jnp.sum(s_mat, axis=-1, keepdims=True) + w_inter * jnp.sum(q * n_prev, axis=-1, keepdims=True)
        hh = num / jnp.maximum(jnp.abs(den), jnp.exp(-m_t))

        b_last = b_col[L - 1:L, :]
        w_log = b_last - b_col + i_col
        m_new = jnp.maximum(b_last + m_prev, jnp.max(w_log, axis=0, keepdims=True))
        decay = jnp.exp(b_last + m_prev - m_new)
        kw = k * jnp.exp(w_log - m_new)
        ct_ref[h] = decay * ct + _dot_tn(kw.astype(BF16), vb)
        n_ref[h] = decay * n_prev + jnp.sum(kw, axis=0, keepdims=True)
        m_ref[h] = jnp.broadcast_to(m_new, m_ref.shape[1:])

        mu = jnp.mean(hh, axis=-1, keepdims=True)
        hc = hh - mu
        var = jnp.mean(hc * hc, axis=-1, keepdims=True)
        hn = hc * lax.rsqrt(var + LN_EPS) * ng_ref[:, h * ML_DV:(h + 1) * ML_DV]
        zlo = 2 * ML_QK + ML_INNER + h * ML_DV
        z = _dot(xb, win_ref[zlo // WCHUNK]) + bin_ref[:, zlo:zlo + ML_DV]
        hb_ref[c * L:(c + 1) * L, h * ML_DV:(h + 1) * ML_DV] = (hn * jax.nn.sigmoid(z)).astype(BF16)

    def out_half(c, half):
        return _dot(hb_ref[c * L:(c + 1) * L, :], wout_ref[half])

    def finish(c, halves):
        rows = slice(c * L, (c + 1) * L)
        y = jnp.concatenate(halves, axis=1)
        o_ref[rows, :] = _layer_norm(DN_ALPHA * x_ref[rows, :] + y, lng_ref[1:2, :], lnb_ref[1:2, :])

    n_pieces = 2 * ML_QK // WCHUNK
    for piece in range(n_pieces):
        qk_piece(0, piece)
    gate_vals = [gates(c) for c in range(n_chunks)]
    for c in range(n_chunks):
        halves = []
        for h in range(ML_HEADS):
            head(c, h, gate_vals[c])
            if c + 1 < n_chunks:
                qk_piece(c + 1, h)
            if c > 0 and h < 2:
                halves.append(out_half(c - 1, h))
                if h == 1:
                    finish(c - 1, halves)
    finish(n_chunks - 1, [out_half(n_chunks - 1, 0), out_half(n_chunks - 1, 1)])

    qk_ref[:, 0:HALO, :] = qk_ref[:, n_chunks * L:n_chunks * L + HALO, :]


def _mlstm_call(x, w_in, b_in, wg_row, bg_row, conv_w, norm_g, w_out, ln_g, ln_b, layer, j, cast_weights):
    S, D = x.shape
    L = ML_ROWS
    main_specs = [
        _resident((ML_MAIN // WCHUNK, D, WCHUNK), (0, 0, 0)),
        _resident((None, 1, ML_MAIN), (j, 0, 0)),
        _resident((None, 4 * ML_HEADS, D), (j, 0, 0)),
        _resident((None, 4 * ML_HEADS, 1), (j, 0, 0)),
        _resident((None, ML_CONV, 2 * ML_QK), (j, 0, 0)),
        _resident((None, 1, ML_INNER), (j, 0, 0)),
        _resident((D // WCHUNK, ML_INNER, WCHUNK), (0, 0, 0)),
        _resident((None, 4, D), (layer, 0, 0)),
        _resident((None, 4, D), (layer, 0, 0)),
    ]
    main_args = [w_in, b_in, wg_row, bg_row, conv_w, norm_g, w_out, ln_g, ln_b]
    scratch = [
        pltpu.VMEM((2 * ML_QK // V7X_LANES, V7X_SUBLANES + ML_TILE, V7X_LANES), F32),
        pltpu.VMEM((2 * ML_QK // V7X_LANES, ML_TILE, V7X_LANES), F32),
        pltpu.VMEM((ML_HEADS, ML_DQK, ML_DV), F32),
        pltpu.VMEM((ML_HEADS, 1, ML_DQK), F32),
        pltpu.VMEM((ML_HEADS, 1, 128), F32),
        pltpu.VMEM((ML_TILE, ML_INNER), BF16),
    ]
    return _call(_mlstm_kernel, "mlstm", x, pl.BlockSpec((ML_TILE, D), lambda t: (t, 0)),
                 main_specs, main_args, cast_weights, scratch)


def _group_cols(w_ref, col0):
    off = col0 % WCHUNK
    return w_ref[col0 // WCHUNK, :, off:off + GM_GDIM]


def _gmlp_kernel(n_in, n_cast, *refs):
    ins, cast_in, o_ref, cast_out, (v_ref, u_ref, y_ref) = _split_refs(refs, n_in, n_cast)
    x_ref, win_ref, bin_ref, vng_ref, vnb_ref, ws_ref, bst_ref, wout_ref, lng_ref, lnb_ref = ins
    _run_casts(cast_in, cast_out)
    xb = x_ref[...].astype(BF16)
    for g in range(GM_GROUPS):
        lo = g * GM_GDIM
        vlo = GM_WIDTH + lo
        v_ref[:, lo:lo + GM_GDIM] = _gelu(_dot(xb, _group_cols(win_ref, vlo)) + bin_ref[:, vlo:vlo + GM_GDIM])
    for g in range(GM_GROUPS):
        lo = g * GM_GDIM
        u_ref[:, lo:lo + GM_GDIM] = _gelu(_dot(xb, _group_cols(win_ref, lo)) + bin_ref[:, lo:lo + GM_GDIM])
    v = v_ref[...]
    mu = jnp.mean(v, axis=-1, keepdims=True)
    vc = v - mu
    rstd = lax.rsqrt(jnp.mean(vc * vc, axis=-1, keepdims=True) + LN_EPS)

    blk_r = lax.broadcasted_iota(jnp.int32, (GM_BLOCK, GM_BLOCK), 0) // CHUNK
    blk_c = lax.broadcasted_iota(jnp.int32, (GM_BLOCK, GM_BLOCK), 1) // CHUNK
    chunk_causal = blk_r >= blk_c
    for g in range(GM_GROUPS):
        lo = g * GM_GDIM
        vn = ((v_ref[:, lo:lo + GM_GDIM] - mu) * rstd * vng_ref[:, lo:lo + GM_GDIM]
              + vnb_ref[:, lo:lo + GM_GDIM]).astype(BF16)
        wsm = jnp.where(chunk_causal, ws_ref[g], 0.0).astype(BF16)
        bias = bst_ref[:, g:g + 1]
        for nb in range(GM_ROWS // GM_BLOCK):
            r0 = nb * GM_BLOCK
            s = _dot(wsm, vn[r0:r0 + GM_BLOCK, :]) + bias
            y_ref[r0:r0 + GM_BLOCK, lo:lo + GM_GDIM] = (u_ref[r0:r0 + GM_BLOCK, lo:lo + GM_GDIM] * s).astype(BF16)
    for r in range(GM_ROWS // GM_SUB):
        rs = slice(r * GM_SUB, (r + 1) * GM_SUB)
        y = _dot_chunked(y_ref[rs, :], wout_ref)
        o_ref[rs, :] = _layer_norm(DN_ALPHA * x_ref[rs, :] + y, lng_ref[1:2, :], lnb_ref[1:2, :])


def _gmlp_call(x, w_in, b_in, vn_g, vn_b, ws, bs_t, w_out, ln_g, ln_b, layer, j, cast_weights):
    S, D = x.shape
    main_specs = [
        _resident((2 * GM_WIDTH // WCHUNK, D, WCHUNK), (0, 0, 0)),
        _resident((None, 1, 2 * GM_WIDTH), (j, 0, 0)),
        _resident((None, 1, GM_WIDTH), (j, 0, 0)),
        _resident((None, 1, GM_WIDTH), (j, 0, 0)),
        _resident((None, GM_GROUPS, GM_BLOCK, GM_BLOCK), (j, 0, 0, 0)),
        _resident((None, GM_BLOCK, GM_GROUPS), (j, 0, 0)),
        _resident((D // WCHUNK, GM_WIDTH, WCHUNK), (0, 0, 0)),
        _resident((None, 4, D), (layer, 0, 0)),
        _resident((None, 4, D), (layer, 0, 0)),
    ]
    main_args = [w_in, b_in, vn_g, vn_b, ws, bs_t, w_out, ln_g, ln_b]
    scratch = [
        pltpu.VMEM((GM_ROWS, GM_WIDTH), F32),
        pltpu.VMEM((GM_ROWS, GM_WIDTH), F32),
        pltpu.VMEM((GM_ROWS, GM_WIDTH), BF16),
    ]
    return _call(_gmlp_kernel, "gmlp", x, pl.BlockSpec((GM_ROWS, D), lambda t: (t, 0)),
                 main_specs, main_args, cast_weights, scratch)


def kernel(x, p, ln_g, ln_b, ffn1_wgu, ffn1_wd, ffn2_wgu, ffn2_wd, ml_w_in, ml_b_in, ml_conv, ml_norm_g, ml_w_out, gm_w_in, gm_b_in, gm_vn_g, gm_vn_b, gm_ws, gm_bs, gm_w_out, ple_wp, ple_wg, ple_bg):
    B, S, D = x.shape
    assert (B, S, D) == (1, SEQ, D_MODEL)
    xs = x.reshape(S, D)
    ps = p.reshape(DEPTH, S, PLE_DIM)

    ple_bg_r = ple_bg[:, None, :]
    ml_b_main = ml_b_in[:, None, :ML_MAIN]
    ml_wg_row = jnp.pad(jnp.swapaxes(ml_w_in[:, :, ML_MAIN:], 1, 2),
                        ((0, 0), (0, 2 * ML_HEADS), (0, 0))).astype(BF16)
    ml_bg_row = jnp.pad(ml_b_in[:, ML_MAIN:], ((0, 0), (0, 2 * ML_HEADS)))[:, :, None]
    ml_norm_g_r = ml_norm_g[:, None, :]
    ml_w_in_t = jnp.swapaxes(ml_w_in, 1, 2)
    gm_b_in_r = gm_b_in[:, None, :]
    gm_vn_g_r, gm_vn_b_r = gm_vn_g[:, None, :], gm_vn_b[:, None, :]
    gm_bs_t = jnp.swapaxes(gm_bs, 1, 2)

    wd0 = ffn1_wd[0].astype(BF16)
    ffn1_w = (ffn1_wgu[0].astype(BF16),
              jnp.stack([wd0[:, c * WCHUNK:(c + 1) * WCHUNK] for c in range(D_MODEL // WCHUNK)]))
    for i in range(DEPTH):
        j = i // 2
        if i % 2 == 0:
            mixer_f32 = [("transposed", ml_w_in_t, j), ("chunked", ml_w_out, j)]
        else:
            mixer_f32 = [("chunked", gm_w_in, j), ("chunked", gm_w_out, j)]
        ffn2_rest_f32 = [("chunked", ffn2_wd, i), ("chunked", ple_wg, i), ("plain", ple_wp, i)]
        xs, cast_w = _ffn_call(xs, *ffn1_w, ln_g, ln_b, i, 0, mixer_f32 + ffn2_rest_f32)
        mixer_w, ffn2_rest_w = cast_w[:2], cast_w[2:]
        ffn2_f32 = [("plain", ffn2_wgu, i)]
        if i % 2 == 0:
            xs, ffn2_w = _mlstm_call(xs, mixer_w[0], ml_b_main, ml_wg_row, ml_bg_row,
                                     ml_conv, ml_norm_g_r, mixer_w[1], ln_g, ln_b, i, j, ffn2_f32)
        else:
            xs, ffn2_w = _gmlp_call(xs, mixer_w[0], gm_b_in_r, gm_vn_g_r, gm_vn_b_r, gm_ws, gm_bs_t,
                                    mixer_w[1], ln_g, ln_b, i, j, ffn2_f32)
        next_f32 = [("plain", ffn1_wgu, i + 1), ("chunked", ffn1_wd, i + 1)] if i + 1 < DEPTH else []
        xs, ffn1_w = _ffn_call(xs, ffn2_w[0], ffn2_rest_w[0], ln_g, ln_b, i, 2, next_f32,
                               ple=(ps, ffn2_rest_w[1], ple_bg_r, ffn2_rest_w[2]))
    return xs.reshape(B, S, D)
```

```python
import functools

import jax
import jax.numpy as jnp
from jax import lax
from jax.experimental import pallas as pl
from jax.experimental.pallas import tpu as pltpu

D_MODEL = 1024
SEQ = 16384
DEPTH = 4
CHUNK = 64
PLE_DIM = 256
FFN_HIDDEN = 2816
ML_HEADS = 4
ML_DQK = 256
ML_DV = 512
ML_INNER = ML_HEADS * ML_DV
ML_QK = ML_HEADS * ML_DQK
ML_MAIN = 2 * ML_QK + 2 * ML_INNER
ML_CONV = 4
GM_BLOCK = 128
GM_GROUPS = 8
GM_WIDTH = 2 * D_MODEL
GM_GDIM = GM_WIDTH // GM_GROUPS
DN_ALPHA = (2.0 * DEPTH) ** 0.25
LN_EPS = 1e-5

V7X_MXU_DIM = 256
V7X_SUBLANES = 8
V7X_LANES = 128
V7X_BF16_ROWS = 16
V7X_VMEM_BYTES = 64 * 1024 * 1024
VMEM_LIMIT_BYTES = V7X_VMEM_BYTES * 7 // 8

FFN_ROWS = 1024
FFN_PLE_ROWS = 1024
FFN_COLS = V7X_MXU_DIM
FFN_SUB = 256
ML_ROWS = 256
ML_TILE = 512
GM_ROWS = 1024
GM_SUB = 256
WCHUNK = 512

F32 = jnp.float32
BF16 = jnp.bfloat16


def _dot(a, b):
    return jnp.dot(a, b, preferred_element_type=F32)


def _dot_nt(a, b):
    return lax.dot_general(a, b, (((1,), (1,)), ((), ())), preferred_element_type=F32)


def _dot_tn(a, b):
    return lax.dot_general(a, b, (((0,), (0,)), ((), ())), preferred_element_type=F32)


def _dot_chunked(a, w_ref):
    return jnp.concatenate([_dot(a, w_ref[c]) for c in range(w_ref.shape[0])], axis=1)


def _layer_norm(z, g, b):
    mu = jnp.mean(z, axis=-1, keepdims=True)
    zc = z - mu
    var = jnp.mean(zc * zc, axis=-1, keepdims=True)
    return zc * lax.rsqrt(var + LN_EPS) * g + b


def _split3(a):
    hi = a.astype(BF16)
    r1 = a - hi.astype(F32)
    mid = r1.astype(BF16)
    lo = (r1 - mid.astype(F32)).astype(BF16)
    return hi, mid, lo


def _gelu(x):
    return 0.5 * x * (1.0 + lax.erf(x * (2.0 ** -0.5)))


def _log_sigmoid(x):
    return jnp.minimum(x, 0.0) - jnp.log1p(jnp.exp(-jnp.abs(x)))


def _resident(shape, index):
    return pl.BlockSpec(shape, lambda t: index, pipeline_mode=pl.Buffered(1))


def _cast_plan(weights, steps):
    in_specs, out_specs, out_shapes, args = [], [], [], []
    for kind, w, layer in weights:
        args.append(w)
        if kind == "transposed":
            _, N, K = w.shape
            last = N // WCHUNK - 1
            assert steps > last
            in_specs.append(pl.BlockSpec((None, WCHUNK, K),
                                         lambda t, layer=layer, last=last: (layer, jnp.minimum(t, last), 0)))
            out_specs.append(pl.BlockSpec((None, K, WCHUNK), lambda t, last=last: (jnp.minimum(t, last), 0, 0)))
            out_shapes.append(jax.ShapeDtypeStruct((last + 1, K, WCHUNK), BF16))
            continue
        _, K, N = w.shape
        hold = 1
        while K % (steps // hold) or (K // (steps // hold)) % V7X_BF16_ROWS:
            hold *= 2
        rows = K // (steps // hold)
        in_specs.append(pl.BlockSpec((None, rows, N), lambda t, layer=layer, hold=hold: (layer, t // hold, 0)))
        if kind == "chunked":
            n = N // WCHUNK
            out_specs.append(pl.BlockSpec((n, rows, WCHUNK), lambda t, hold=hold: (0, t // hold, 0)))
            out_shapes.append(jax.ShapeDtypeStruct((n, K, WCHUNK), BF16))
        else:
            out_specs.append(pl.BlockSpec((rows, N), lambda t, hold=hold: (t // hold, 0)))
            out_shapes.append(jax.ShapeDtypeStruct((K, N), BF16))
    return in_specs, out_specs, out_shapes, args


def _split_refs(refs, n_in, n_cast):
    a, b, c = n_in, n_in + n_cast, n_in + 2 * n_cast + 1
    return refs[:a], refs[a:b], refs[b], refs[b + 1:c], refs[c:]


def _run_casts(cast_in, cast_out):
    for src, dst in zip(cast_in, cast_out):
        if len(dst.shape) == 3:
            width = dst.shape[2]
            for c in range(dst.shape[0]):
                dst[c] = src[:, c * width:(c + 1) * width].astype(BF16)
        elif dst.shape != src.shape:
            dst[...] = src[...].T.astype(BF16)
        else:
            dst[...] = src[...].astype(BF16)


def _call(kernel_fn, name, x, row_spec, main_specs, main_args, cast_weights, scratch_shapes):
    S, D = x.shape
    steps = S // row_spec.block_shape[0]
    c_in, c_out, c_shapes, c_args = _cast_plan(cast_weights, steps)
    outs = pl.pallas_call(
        functools.partial(kernel_fn, 1 + len(main_args), len(c_args)),
        grid=(steps,),
        in_specs=[row_spec] + main_specs + c_in,
        out_specs=[row_spec] + c_out,
        out_shape=[jax.ShapeDtypeStruct((S, D), F32)] + c_shapes,
        scratch_shapes=scratch_shapes,
        compiler_params=pltpu.CompilerParams(
            dimension_semantics=("arbitrary",), vmem_limit_bytes=VMEM_LIMIT_BYTES),
        name=name,
    )(x, *main_args, *c_args)
    return outs[0], outs[1:]


def _ffn_kernel(with_ple, ln_row, n_in, n_cast, *refs):
    ins, cast_in, o_ref, cast_out, (a_ref,) = _split_refs(refs, n_in, n_cast)
    if with_ple:
        x_ref, wgu_ref, wd_ref, lng_ref, lnb_ref, p_ref, wpg_ref, bpg_ref, wpp_ref = ins
    else:
        x_ref, wgu_ref, wd_ref, lng_ref, lnb_ref = ins
    _run_casts(cast_in, cast_out)
    xb = x_ref[...].astype(BF16)
    for c in range(FFN_HIDDEN // FFN_COLS):
        lo = c * FFN_COLS
        g = _dot(xb, wgu_ref[:, lo:lo + FFN_COLS])
        u = _dot(xb, wgu_ref[:, FFN_HIDDEN + lo:FFN_HIDDEN + lo + FFN_COLS])
        a_ref[:, lo:lo + FFN_COLS] = (jax.nn.silu(g) * u).astype(BF16)
    n_sub = a_ref.shape[0] // FFN_SUB

    def down(r):
        rs = slice(r * FFN_SUB, (r + 1) * FFN_SUB)
        y = _dot_chunked(a_ref[rs, :], wd_ref)
        return _layer_norm(DN_ALPHA * x_ref[rs, :] + 0.5 * y,
                           lng_ref[ln_row:ln_row + 1, :], lnb_ref[ln_row:ln_row + 1, :])

    x1 = down(0)
    for r in range(n_sub):
        rs = slice(r * FFN_SUB, (r + 1) * FFN_SUB)
        x1_next = down(r + 1) if r + 1 < n_sub else None
        if with_ple:
            gate = jax.nn.sigmoid(_dot_chunked(x1.astype(BF16), wpg_ref) + bpg_ref[...])
            emb = _dot(p_ref[rs, :].astype(BF16), wpp_ref[...])
            x1 = _layer_norm(DN_ALPHA * x1 + gate * emb, lng_ref[3:4, :], lnb_ref[3:4, :])
        o_ref[rs, :] = x1
        x1 = x1_next


def _ffn_call(x, wgu, wd, ln_g, ln_b, layer, ln_row, cast_weights, ple=None):
    S, D = x.shape
    F = FFN_HIDDEN
    tm = FFN_ROWS if ple is None else FFN_PLE_ROWS
    main_specs = [
        _resident((D, 2 * F), (0, 0)),
        _resident((D // WCHUNK, F, WCHUNK), (0, 0, 0)),
        _resident((None, 4, D), (layer, 0, 0)),
        _resident((None, 4, D), (layer, 0, 0)),
    ]
    main_args = [wgu, wd, ln_g, ln_b]
    if ple is not None:
        p, wpg, bpg, wpp = ple
        main_specs += [
            pl.BlockSpec((None, tm, PLE_DIM), lambda t: (layer, t, 0)),
            _resident((D // WCHUNK, D, WCHUNK), (0, 0, 0)),
            _resident((None, 1, D), (layer, 0, 0)),
            _resident((PLE_DIM, D), (0, 0)),
        ]
        main_args += [p, wpg, bpg, wpp]
    return _call(functools.partial(_ffn_kernel, ple is not None, ln_row),
                 "ffn_ple" if ple is not None else "ffn",
                 x, pl.BlockSpec((tm, D), lambda t: (t, 0)), main_specs, main_args, cast_weights,
                 [pltpu.VMEM((tm, F), BF16)])


def _conv_silu(qk_ref, act_ref, conv_ref, row0, col0, scale):
    L = ML_ROWS
    HALO = V7X_SUBLANES
    groups = L // V7X_SUBLANES
    outs = []
    for b in range(ML_DQK // V7X_LANES):
        blk = col0 // V7X_LANES + b
        acc = None
        for j in range(ML_CONV):
            start = row0 + HALO - (ML_CONV - 1) + j
            tap = jnp.concatenate(
                [qk_ref[blk, pl.ds(start + s, groups, stride=V7X_SUBLANES), :] for s in range(V7X_SUBLANES)], axis=0)
            term = conv_ref[j:j + 1, blk * V7X_LANES:(blk + 1) * V7X_LANES] * tap
            acc = term if acc is None else acc + term
        act = jax.nn.silu(acc) * scale if scale != 1.0 else jax.nn.silu(acc)
        for s in range(V7X_SUBLANES):
            act_ref[blk, pl.ds(row0 + s, groups, stride=V7X_SUBLANES), :] = act[s * groups:(s + 1) * groups, :]
        outs.append(act_ref[blk, row0:row0 + L, :])
    return jnp.concatenate(outs, axis=1)


def _mlstm_kernel(n_in, n_cast, *refs):
    ins, cast_in, o_ref, cast_out, scratch = _split_refs(refs, n_in, n_cast)
    (x_ref, win_ref, bin_ref, wgr_ref, bgr_ref, conv_ref, ng_ref,
     wout_ref, lng_ref, lnb_ref) = ins
    qk_ref, act_ref, ct_ref, n_ref, m_ref, hb_ref = scratch
    L = ML_ROWS
    HALO = V7X_SUBLANES
    n_chunks = x_ref.shape[0] // L
    n_blk = WCHUNK // V7X_LANES

    @pl.when(pl.program_id(0) == 0)
    def _init():
        qk_ref[:, 0:HALO, :] = jnp.zeros((2 * ML_QK // V7X_LANES, HALO, V7X_LANES), F32)
        ct_ref[...] = jnp.zeros_like(ct_ref)
        n_ref[...] = jnp.zeros_like(n_ref)
        m_ref[...] = jnp.zeros_like(m_ref)

    _run_casts(cast_in, cast_out)
    xbs = [x_ref[c * L:(c + 1) * L, :].astype(BF16) for c in range(n_chunks)]

    ri = lax.broadcasted_iota(jnp.int32, (L, L), 0)
    ci = lax.broadcasted_iota(jnp.int32, (L, L), 1)
    causal = ri >= ci
    upper = (ri <= ci).astype(BF16)

    def qk_piece(c, piece):
        pre = _dot(xbs[c], win_ref[piece]) + bin_ref[:, piece * WCHUNK:(piece + 1) * WCHUNK]
        for b in range(n_blk):
            qk_ref[piece * n_blk + b, HALO + c * L:HALO + (c + 1) * L, :] = pre[:, b * V7X_LANES:(b + 1) * V7X_LANES]

    def gates(c):
        g_row = _dot_nt(wgr_ref[...], xbs[c]) + bgr_ref[...]
        n_rows = g_row.shape[0]
        parts = _dot(jnp.concatenate(_split3(_log_sigmoid(g_row)), axis=0), upper)
        bcum_row = parts[0:n_rows] + parts[n_rows:2 * n_rows] + parts[2 * n_rows:]
        pad = jnp.zeros((V7X_LANES - 2 * n_rows, L), F32)
        cols = jnp.concatenate([g_row, bcum_row, pad], axis=0).T
        return cols[:, 0:n_rows], g_row, cols[:, n_rows:2 * n_rows], bcum_row

    def head(c, h, gate_vals):
        g_col, g_row, bcum_col, bcum_row = gate_vals
        xb = xbs[c]
        q = _conv_silu(qk_ref, act_ref, conv_ref, c * L, h * ML_DQK, 1.0)
        k = _conv_silu(qk_ref, act_ref, conv_ref, c * L, ML_QK + h * ML_DQK, ML_DQK ** -0.5)
        qb = q.astype(BF16)
        kb = k.astype(BF16)
        vlo = 2 * ML_QK + h * ML_DV
        v = _dot(xb, win_ref[vlo // WCHUNK]) + bin_ref[:, vlo:vlo + ML_DV]
        vb = v.astype(BF16)

        i_col = g_col[:, h:h + 1]
        b_col = bcum_col[:, ML_HEADS + h:ML_HEADS + h + 1]
        i_row = g_row[h:h + 1, :]
        b_row = bcum_row[ML_HEADS + h:ML_HEADS + h + 1, :]
        m_prev = m_ref[h][:, 0:1]
        ct = ct_ref[h]
        n_prev = n_ref[h]

        d_log = jnp.where(causal, b_col - b_row + i_row, -jnp.inf)
        inter = b_col + m_prev
        m_t = jnp.maximum(inter, jnp.max(d_log, axis=-1, keepdims=True))
        s_mat = _dot_nt(qb, kb) * jnp.exp(d_log - m_t)
        w_inter = jnp.exp(inter - m_t)
        num = _dot(s_mat.astype(BF16), vb) + w_inter * _dot(qb, ct.astype(BF16))
        den = jnp.sum(s_mat, axis=-1, keepdims=True) + w_inter * jnp.sum(q * n_prev, axis=-1, keepdims=True)
        hh = num / jnp.maximum(jnp.abs(den), jnp.exp(-m_t))

        b_last = b_col[L - 1:L, :]
        w_log = b_last - b_col + i_col
        m_new = jnp.maximum(b_last + m_prev, jnp.max(w_log, axis=0, keepdims=True))
        decay = jnp.exp(b_last + m_prev - m_new)
        kw = k * jnp.exp(w_log - m_new)
        ct_ref[h] = decay * ct + _dot_tn(kw.astype(BF16), vb)
        n_ref[h] = decay * n_prev + jnp.sum(kw, axis=0, keepdims=True)
        m_ref[h] = jnp.broadcast_to(m_new, m_ref.shape[1:])

        mu = jnp.mean(hh, axis=-1, keepdims=True)
        hc = hh - mu
        var = jnp.mean(hc * hc, axis=-1, keepdims=True)
        hn = hc * lax.rsqrt(var + LN_EPS) * ng_ref[:, h * ML_DV:(h + 1) * ML_DV]
        zlo = 2 * ML_QK + ML_INNER + h * ML_DV
        z = _dot(xb, win_ref[zlo // WCHUNK]) + bin_ref[:, zlo:zlo + ML_DV]
        hb_ref[c * L:(c + 1) * L, h * ML_DV:(h + 1) * ML_DV] = (hn * jax.nn.sigmoid(z)).astype(BF16)

    def out_half(c, half):
        return _dot(hb_ref[c * L:(c + 1) * L, :], wout_ref[half])

    def finish(c, halves):
        rows = slice(c * L, (c + 1) * L)
        y = jnp.concatenate(halves, axis=1)
        o_ref[rows, :] = _layer_norm(DN_ALPHA * x_ref[rows, :] + y, lng_ref[1:2, :], lnb_ref[1:2, :])

    n_pieces = 2 * ML_QK // WCHUNK
    for piece in range(n_pieces):
        qk_piece(0, piece)
    gate_vals = [gates(c) for c in range(n_chunks)]
    for c in range(n_chunks):
        halves = []
        for h in range(ML_HEADS):
            head(c, h, gate_vals[c])
            if c + 1 < n_chunks:
                qk_piece(c + 1, h)
            if c > 0 and h < 2:
                halves.append(out_half(c - 1, h))
                if h == 1:
                    finish(c - 1, halves)
    finish(n_chunks - 1, [out_half(n_chunks - 1, 0), out_half(n_chunks - 1, 1)])

    qk_ref[:, 0:HALO, :] = qk_ref[:, n_chunks * L:n_chunks * L + HALO, :]


def _mlstm_call(x, w_in, b_in, wg_row, bg_row, conv_w, norm_g, w_out, ln_g, ln_b, layer, j, cast_weights):
    S, D = x.shape
    L = ML_ROWS
    main_specs = [
        _resident((ML_MAIN // WCHUNK, D, WCHUNK), (0, 0, 0)),
        _resident((None, 1, ML_MAIN), (j, 0, 0)),
        _resident((None, 4 * ML_HEADS, D), (j, 0, 0)),
        _resident((None, 4 * ML_HEADS, 1), (j, 0, 0)),
        _resident((None, ML_CONV, 2 * ML_QK), (j, 0, 0)),
        _resident((None, 1, ML_INNER), (j, 0, 0)),
        _resident((D // WCHUNK, ML_INNER, WCHUNK), (0, 0, 0)),
        _resident((None, 4, D), (layer, 0, 0)),
        _resident((None, 4, D), (layer, 0, 0)),
    ]
    main_args = [w_in, b_in, wg_row, bg_row, conv_w, norm_g, w_out, ln_g, ln_b]
    scratch = [
        pltpu.VMEM((2 * ML_QK // V7X_LANES, V7X_SUBLANES + ML_TILE, V7X_LANES), F32),
        pltpu.VMEM((2 * ML_QK // V7X_LANES, ML_TILE, V7X_LANES), F32),
        pltpu.VMEM((ML_HEADS, ML_DQK, ML_DV), F32),
        pltpu.VMEM((ML_HEADS, 1, ML_DQK), F32),
        pltpu.VMEM((ML_HEADS, 1, 128), F32),
        pltpu.VMEM((ML_TILE, ML_INNER), BF16),
    ]
    return _call(_mlstm_kernel, "mlstm", x, pl.BlockSpec((ML_TILE, D), lambda t: (t, 0)),
                 main_specs, main_args, cast_weights, scratch)


def _group_cols(w_ref, col0):
    off = col0 % WCHUNK
    return w_ref[col0 // WCHUNK, :, off:off + GM_GDIM]


def _gmlp_kernel(n_in, n_cast, *refs):
    ins, cast_in, o_ref, cast_out, (v_ref, u_ref, y_ref) = _split_refs(refs, n_in, n_cast)
    x_ref, win_ref, bin_ref, vng_ref, vnb_ref, ws_ref, bst_ref, wout_ref, lng_ref, lnb_ref = ins
    _run_casts(cast_in, cast_out)
    xb = x_ref[...].astype(BF16)
    for g in range(GM_GROUPS):
        lo = g * GM_GDIM
        vlo = GM_WIDTH + lo
        v_ref[:, lo:lo + GM_GDIM] = _gelu(_dot(xb, _group_cols(win_ref, vlo)) + bin_ref[:, vlo:vlo + GM_GDIM])
    for g in range(GM_GROUPS):
        lo = g * GM_GDIM
        u_ref[:, lo:lo + GM_GDIM] = _gelu(_dot(xb, _group_cols(win_ref, lo)) + bin_ref[:, lo:lo + GM_GDIM])
    v = v_ref[...]
    mu = jnp.mean(v, axis=-1, keepdims=True)
    vc = v - mu
    rstd = lax.rsqrt(jnp.mean(vc * vc, axis=-1, keepdims=True) + LN_EPS)

    blk_r = lax.broadcasted_iota(jnp.int32, (GM_BLOCK, GM_BLOCK), 0) // CHUNK
    blk_c = lax.broadcasted_iota(jnp.int32, (GM_BLOCK, GM_BLOCK), 1) // CHUNK
    chunk_causal = blk_r >= blk_c
    for g in range(GM_GROUPS):
        lo = g * GM_GDIM
        vn = ((v_ref[:, lo:lo + GM_GDIM] - mu) * rstd * vng_ref[:, lo:lo + GM_GDIM]
              + vnb_ref[:, lo:lo + GM_GDIM]).astype(BF16)
        wsm = jnp.where(chunk_causal, ws_ref[g], 0.0).astype(BF16)
        bias = bst_ref[:, g:g + 1]
        for nb in range(GM_ROWS // GM_BLOCK):
            r0 = nb * GM_BLOCK
            s = _dot(wsm, vn[r0:r0 + GM_BLOCK, :]) + bias
            y_ref[r0:r0 + GM_BLOCK, lo:lo + GM_GDIM] = (u_ref[r0:r0 + GM_BLOCK, lo:lo + GM_GDIM] * s).astype(BF16)
    for r in range(GM_ROWS // GM_SUB):
        rs = slice(r * GM_SUB, (r + 1) * GM_SUB)
        y = _dot_chunked(y_ref[rs, :], wout_ref)
        o_ref[rs, :] = _layer_norm(DN_ALPHA * x_ref[rs, :] + y, lng_ref[1:2, :], lnb_ref[1:2, :])


def _gmlp_call(x, w_in, b_in, vn_g, vn_b, ws, bs_t, w_out, ln_g, ln_b, layer, j, cast_weights):
    S, D = x.shape
    main_specs = [
        _resident((2 * GM_WIDTH // WCHUNK, D, WCHUNK), (0, 0, 0)),
        _resident((None, 1, 2 * GM_WIDTH), (j, 0, 0)),
        _resident((None, 1, GM_WIDTH), (j, 0, 0)),
        _resident((None, 1, GM_WIDTH), (j, 0, 0)),
        _resident((None, GM_GROUPS, GM_BLOCK, GM_BLOCK), (j, 0, 0, 0)),
        _resident((None, GM_BLOCK, GM_GROUPS), (j, 0, 0)),
        _resident((D // WCHUNK, GM_WIDTH, WCHUNK), (0, 0, 0)),
        _resident((None, 4, D), (layer, 0, 0)),
        _resident((None, 4, D), (layer, 0, 0)),
    ]
    main_args = [w_in, b_in, vn_g, vn_b, ws, bs_t, w_out, ln_g, ln_b]
    scratch = [
        pltpu.VMEM((GM_ROWS, GM_WIDTH), F32),
        pltpu.VMEM((GM_ROWS, GM_WIDTH), F32),
        pltpu.VMEM((GM_ROWS, GM_WIDTH), BF16),
    ]
    return _call(_gmlp_kernel, "gmlp", x, pl.BlockSpec((GM_ROWS, D), lambda t: (t, 0)),
                 main_specs, main_args, cast_weights, scratch)


def kernel(x, p, ln_g, ln_b, ffn1_wgu, ffn1_wd, ffn2_wgu, ffn2_wd, ml_w_in, ml_b_in, ml_conv, ml_norm_g, ml_w_out, gm_w_in, gm_b_in, gm_vn_g, gm_vn_b, gm_ws, gm_bs, gm_w_out, ple_wp, ple_wg, ple_bg):
    B, S, D = x.shape
    assert (B, S, D) == (1, SEQ, D_MODEL)
    xs = x.reshape(S, D)
    ps = p.reshape(DEPTH, S, PLE_DIM)

    ple_bg_r = ple_bg[:, None, :]
    ml_b_main = ml_b_in[:, None, :ML_MAIN]
    ml_wg_row = jnp.pad(jnp.swapaxes(ml_w_in[:, :, ML_MAIN:], 1, 2),
                        ((0, 0), (0, 2 * ML_HEADS), (0, 0))).astype(BF16)
    ml_bg_row = jnp.pad(ml_b_in[:, ML_MAIN:], ((0, 0), (0, 2 * ML_HEADS)))[:, :, None]
    ml_norm_g_r = ml_norm_g[:, None, :]
    ml_w_in_t = jnp.swapaxes(ml_w_in, 1, 2)
    gm_b_in_r = gm_b_in[:, None, :]
    gm_vn_g_r, gm_vn_b_r = gm_vn_g[:, None, :], gm_vn_b[:, None, :]
    gm_bs_t = jnp.swapaxes(gm_bs, 1, 2)

    wd0 = ffn1_wd[0].astype(BF16)
    ffn1_w = (ffn1_wgu[0].astype(BF16),
              jnp.stack([wd0[:, c * WCHUNK:(c + 1) * WCHUNK] for c in range(D_MODEL // WCHUNK)]))
    for i in range(DEPTH):
        j = i // 2
        if i % 2 == 0:
            mixer_f32 = [("transposed", ml_w_in_t, j), ("chunked", ml_w_out, j)]
        else:
            mixer_f32 = [("chunked", gm_w_in, j), ("chunked", gm_w_out, j)]
        ffn2_rest_f32 = [("chunked", ffn2_wd, i), ("chunked", ple_wg, i), ("plain", ple_wp, i)]
        xs, cast_w = _ffn_call(xs, *ffn1_w, ln_g, ln_b, i, 0, mixer_f32 + ffn2_rest_f32)
        mixer_w, ffn2_rest_w = cast_w[:2], cast_w[2:]
        ffn2_f32 = [("plain", ffn2_wgu, i)]
        if i % 2 == 0:
            xs, ffn2_w = _mlstm_call(xs, mixer_w[0], ml_b_main, ml_wg_row, ml_bg_row,
                                     ml_conv, ml_norm_g_r, mixer_w[1], ln_g, ln_b, i, j, ffn2_f32)
        else:
            xs, ffn2_w = _gmlp_call(xs, mixer_w[0], gm_b_in_r, gm_vn_g_r, gm_vn_b_r, gm_ws, gm_bs_t,
                                    mixer_w[1], ln_g, ln_b, i, j, ffn2_f32)
        next_f32 = [("plain", ffn1_wgu, i + 1), ("chunked", ffn1_wd, i + 1)] if i + 1 < DEPTH else []
        xs, ffn1_w = _ffn_call(xs, ffn2_w[0], ffn2_rest_w[0], ln_g, ln_b, i, 2, next_f32,
                               ple=(ps, ffn2_rest_w[1], ple_bg_r, ffn2_rest_w[2]))
    return xs.reshape(B, S, D)
```

```python
import functools

import jax
import jax.numpy as jnp
from jax import lax
from jax.experimental import pallas as pl
from jax.experimental.pallas import tpu as pltpu

D_MODEL = 1024
SEQ = 16384
DEPTH = 4
CHUNK = 64
PLE_DIM = 256
FFN_HIDDEN = 2816
ML_HEADS = 4
ML_DQK = 256
ML_DV = 512
ML_INNER = ML_HEADS * ML_DV
ML_QK = ML_HEADS * ML_DQK
ML_MAIN = 2 * ML_QK + 2 * ML_INNER
ML_CONV = 4
GM_BLOCK = 128
GM_GROUPS = 8
GM_WIDTH = 2 * D_MODEL
GM_GDIM = GM_WIDTH // GM_GROUPS
DN_ALPHA = (2.0 * DEPTH) ** 0.25
LN_EPS = 1e-5

V7X_MXU_DIM = 256
V7X_SUBLANES = 8
V7X_LANES = 128
V7X_BF16_ROWS = 16
V7X_VMEM_BYTES = 64 * 1024 * 1024
VMEM_LIMIT_BYTES = V7X_VMEM_BYTES * 7 // 8

FFN_ROWS = 1024
FFN_PLE_ROWS = 1024
FFN_COLS = V7X_MXU_DIM
FFN_SUB = 256
ML_ROWS = 256
ML_TILE = 512
GM_ROWS = 1024
GM_SUB = 256
WCHUNK = 512

F32 = jnp.float32
BF16 = jnp.bfloat16


def _dot(a, b):
    return jnp.dot(a, b, preferred_element_type=F32)


def _dot_nt(a, b):
    return lax.dot_general(a, b, (((1,), (1,)), ((), ())), preferred_element_type=F32)


def _dot_tn(a, b):
    return lax.dot_general(a, b, (((0,), (0,)), ((), ())), preferred_element_type=F32)


def _dot_chunked(a, w_ref):
    return jnp.concatenate([_dot(a, w_ref[c]) for c in range(w_ref.shape[0])], axis=1)


def _layer_norm(z, g, b):
    mu = jnp.mean(z, axis=-1, keepdims=True)
    zc = z - mu
    var = jnp.mean(zc * zc, axis=-1, keepdims=True)
    return zc * lax.rsqrt(var + LN_EPS) * g + b


def _split3(a):
    hi = a.astype(BF16)
    r1 = a - hi.astype(F32)
    mid = r1.astype(BF16)
    lo = (r1 - mid.astype(F32)).astype(BF16)
    return hi, mid, lo


def _gelu(x):
    return 0.5 * x * (1.0 + lax.erf(x * (2.0 ** -0.5)))


def _log_sigmoid(x):
    return jnp.minimum(x, 0.0) - jnp.log1p(jnp.exp(-jnp.abs(x)))


def _resident(shape, index):
    return pl.BlockSpec(shape, lambda t: index, pipeline_mode=pl.Buffered(1))


def _cast_plan(weights, steps):
    in_specs, out_specs, out_shapes, args = [], [], [], []
    for kind, w, layer in weights:
        args.append(w)
        if kind == "transposed":
            _, N, K = w.shape
            last = N // WCHUNK - 1
            assert steps > last
            in_specs.append(pl.BlockSpec((None, WCHUNK, K),
                                         lambda t, layer=layer, last=last: (layer, jnp.minimum(t, last), 0)))
            out_specs.append(pl.BlockSpec((None, K, WCHUNK), lambda t, last=last: (jnp.minimum(t, last), 0, 0)))
            out_shapes.append(jax.ShapeDtypeStruct((last + 1, K, WCHUNK), BF16))
            continue
        _, K, N = w.shape
        hold = 1
        while K % (steps // hold) or (K // (steps // hold)) % V7X_BF16_ROWS:
            hold *= 2
        rows = K // (steps // hold)
        in_specs.append(pl.BlockSpec((None, rows, N), lambda t, layer=layer, hold=hold: (layer, t // hold, 0)))
        if kind == "chunked":
            n = N // WCHUNK
            out_specs.append(pl.BlockSpec((n, rows, WCHUNK), lambda t, hold=hold: (0, t // hold, 0)))
            out_shapes.append(jax.ShapeDtypeStruct((n, K, WCHUNK), BF16))
        else:
            out_specs.append(pl.BlockSpec((rows, N), lambda t, hold=hold: (t // hold, 0)))
            out_shapes.append(jax.ShapeDtypeStruct((K, N), BF16))
    return in_specs, out_specs, out_shapes, args


def _split_refs(refs, n_in, n_cast):
    a, b, c = n_in, n_in + n_cast, n_in + 2 * n_cast + 1
    return refs[:a], refs[a:b], refs[b], refs[b + 1:c], refs[c:]


def _run_casts(cast_in, cast_out):
    for src, dst in zip(cast_in, cast_out):
        if len(dst.shape) == 3:
            width = dst.shape[2]
            for c in range(dst.shape[0]):
                dst[c] = src[:, c * width:(c + 1) * width].astype(BF16)
        elif dst.shape != src.shape:
            dst[...] = src[...].T.astype(BF16)
        else:
            dst[...] = src[...].astype(BF16)


def _call(kernel_fn, name, x, row_spec, main_specs, main_args, cast_weights, scratch_shapes):
    S, D = x.shape
    steps = S // row_spec.block_shape[0]
    c_in, c_out, c_shapes, c_args = _cast_plan(cast_weights, steps)
    outs = pl.pallas_call(
        functools.partial(kernel_fn, 1 + len(main_args), len(c_args)),
        grid=(steps,),
        in_specs=[row_spec] + main_specs + c_in,
        out_specs=[row_spec] + c_out,
        out_shape=[jax.ShapeDtypeStruct((S, D), F32)] + c_shapes,
        scratch_shapes=scratch_shapes,
        compiler_params=pltpu.CompilerParams(
            dimension_semantics=("arbitrary",), vmem_limit_bytes=VMEM_LIMIT_BYTES),
        name=name,
    )(x, *main_args, *c_args)
    return outs[0], outs[1:]


def _ffn_kernel(with_ple, ln_row, n_in, n_cast, *refs):
    ins, cast_in, o_ref, cast_out, (a_ref,) = _split_refs(refs, n_in, n_cast)
    if with_ple:
        x_ref, wgu_ref, wd_ref, lng_ref, lnb_ref, p_ref, wpg_ref, bpg_ref, wpp_ref = ins
    else:
        x_ref, wgu_ref, wd_ref, lng_ref, lnb_ref = ins
    _run_casts(cast_in, cast_out)
    xb = x_ref[...].astype(BF16)
    for c in range(FFN_HIDDEN // FFN_COLS):
        lo = c * FFN_COLS
        g = _dot(xb, wgu_ref[:, lo:lo + FFN_COLS])
        u = _dot(xb, wgu_ref[:, FFN_HIDDEN + lo:FFN_HIDDEN + lo + FFN_COLS])
        a_ref[:, lo:lo + FFN_COLS] = (jax.nn.silu(g) * u).astype(BF16)
    n_sub = a_ref.shape[0] // FFN_SUB

    def down(r):
        rs = slice(r * FFN_SUB, (r + 1) * FFN_SUB)
        y = _dot_chunked(a_ref[rs, :], wd_ref)
        return _layer_norm(DN_ALPHA * x_ref[rs, :] + 0.5 * y,
                           lng_ref[ln_row:ln_row + 1, :], lnb_ref[ln_row:ln_row + 1, :])

    x1 = down(0)
    for r in range(n_sub):
        rs = slice(r * FFN_SUB, (r + 1) * FFN_SUB)
        x1_next = down(r + 1) if r + 1 < n_sub else None
        if with_ple:
            gate = jax.nn.sigmoid(_dot_chunked(x1.astype(BF16), wpg_ref) + bpg_ref[...])
            emb = _dot(p_ref[rs, :].astype(BF16), wpp_ref[...])
            x1 = _layer_norm(DN_ALPHA * x1 + gate * emb, lng_ref[3:4, :], lnb_ref[3:4, :])
        o_ref[rs, :] = x1
        x1 = x1_next


def _ffn_call(x, wgu, wd, ln_g, ln_b, layer, ln_row, cast_weights, ple=None):
    S, D = x.shape
    F = FFN_HIDDEN
    tm = FFN_ROWS if ple is None else FFN_PLE_ROWS
    main_specs = [
        _resident((D, 2 * F), (0, 0)),
        _resident((D // WCHUNK, F, WCHUNK), (0, 0, 0)),
        _resident((None, 4, D), (layer, 0, 0)),
        _resident((None, 4, D), (layer, 0, 0)),
    ]
    main_args = [wgu, wd, ln_g, ln_b]
    if ple is not None:
        p, wpg, bpg, wpp = ple
        main_specs += [
            pl.BlockSpec((None, tm, PLE_DIM), lambda t: (layer, t, 0)),
            _resident((D // WCHUNK, D, WCHUNK), (0, 0, 0)),
            _resident((None, 1, D), (layer, 0, 0)),
            _resident((PLE_DIM, D), (0, 0)),
        ]
        main_args += [p, wpg, bpg, wpp]
    return _call(functools.partial(_ffn_kernel, ple is not None, ln_row),
                 "ffn_ple" if ple is not None else "ffn",
                 x, pl.BlockSpec((tm, D), lambda t: (t, 0)), main_specs, main_args, cast_weights,
                 [pltpu.VMEM((tm, F), BF16)])


def _conv_silu(qk_ref, act_ref, conv_ref, row0, col0, scale):
    L = ML_ROWS
    HALO = V7X_SUBLANES
    groups = L // V7X_SUBLANES
    outs = []
    for b in range(ML_DQK // V7X_LANES):
        blk = col0 // V7X_LANES + b
        acc = None
        for j in range(ML_CONV):
            start = row0 + HALO - (ML_CONV - 1) + j
            tap = jnp.concatenate(
                [qk_ref[blk, pl.ds(start + s, groups, stride=V7X_SUBLANES), :] for s in range(V7X_SUBLANES)], axis=0)
            term = conv_ref[j:j + 1, blk * V7X_LANES:(blk + 1) * V7X_LANES] * tap
            acc = term if acc is None else acc + term
        act = jax.nn.silu(acc) * scale if scale != 1.0 else jax.nn.silu(acc)
        for s in range(V7X_SUBLANES):
            act_ref[blk, pl.ds(row0 + s, groups, stride=V7X_SUBLANES), :] = act[s * groups:(s + 1) * groups, :]
        outs.append(act_ref[blk, row0:row0 + L, :])
    return jnp.concatenate(outs, axis=1)


def _mlstm_kernel(n_in, n_cast, *refs):
    ins, cast_in, o_ref, cast_out, scratch = _split_refs(refs, n_in, n_cast)
    (x_ref, win_ref, bin_ref, wgr_ref, bgr_ref, conv_ref, ng_ref,
     wout_ref, lng_ref, lnb_ref) = ins
    qk_ref, act_ref, ct_ref, n_ref, m_ref, hb_ref, xb_ref = scratch
    L = ML_ROWS
    HALO = V7X_SUBLANES
    n_chunks = x_ref.shape[0] // L
    n_blk = WCHUNK // V7X_LANES

    @pl.when(pl.program_id(0) == 0)
    def _init():
        qk_ref[:, 0:HALO, :] = jnp.zeros((2 * ML_QK // V7X_LANES, HALO, V7X_LANES), F32)
        ct_ref[...] = jnp.zeros_like(ct_ref)
        n_ref[...] = jnp.zeros_like(n_ref)
        m_ref[...] = jnp.zeros_like(m_ref)

    _run_casts(cast_in, cast_out)
    xb_ref[...] = x_ref[...].astype(BF16)

    def xb_of(c):
        return xb_ref[c * L:(c + 1) * L, :]

    ri = lax.broadcasted_iota(jnp.int32, (L, L), 0)
    ci = lax.broadcasted_iota(jnp.int32, (L, L), 1)
    causal = ri >= ci
    upper = (ri <= ci).astype(BF16)

    def qk_piece(c, piece):
        pre = _dot(xb_of(c), win_ref[piece]) + bin_ref[:, piece * WCHUNK:(piece + 1) * WCHUNK]
        for b in range(n_blk):
            qk_ref[piece * n_blk + b, HALO + c * L:HALO + (c + 1) * L, :] = pre[:, b * V7X_LANES:(b + 1) * V7X_LANES]

    def gates(c):
        g_row = _dot_nt(wgr_ref[...], xb_of(c)) + bgr_ref[...]
        n_rows = g_row.shape[0]
        parts = _dot(jnp.concatenate(_split3(_log_sigmoid(g_row)), axis=0), upper)
        bcum_row = parts[0:n_rows] + parts[n_rows:2 * n_rows] + parts[2 * n_rows:]
        pad = jnp.zeros((V7X_LANES - 2 * n_rows, L), F32)
        cols = jnp.concatenate([g_row, bcum_row, pad], axis=0).T
        return cols[:, 0:n_rows], g_row, cols[:, n_rows:2 * n_rows], bcum_row

    def head(c, h, gate_vals):
        g_col, g_row, bcum_col, bcum_row = gate_vals
        q = _conv_silu(qk_ref, act_ref, conv_ref, c * L, h * ML_DQK, 1.0)
        k = _conv_silu(qk_ref, act_ref, conv_ref, c * L, ML_QK + h * ML_DQK, ML_DQK ** -0.5)
        qb = q.astype(BF16)
        kb = k.astype(BF16)
        vlo = 2 * ML_QK + h * ML_DV
        v = _dot(xb_of(c), win_ref[vlo // WCHUNK]) + bin_ref[:, vlo:vlo + ML_DV]
        vb = v.astype(BF16)

        i_col = g_col[:, h:h + 1]
        b_col = bcum_col[:, ML_HEADS + h:ML_HEADS + h + 1]
        i_row = g_row[h:h + 1, :]
        b_row = bcum_row[ML_HEADS + h:ML_HEADS + h + 1, :]
        m_prev = m_ref[h][:, 0:1]
        ct = ct_ref[h]
        n_prev = n_ref[h]

        d_log = jnp.where(causal, b_col - b_row + i_row, -jnp.inf)
        inter = b_col + m_prev
        m_t = jnp.maximum(inter, jnp.max(d_log, axis=-1, keepdims=True))
        s_mat = _dot_nt(qb, kb) * jnp.exp(d_log - m_t)
        w_inter = jnp.exp(inter - m_t)
        num = _dot(s_mat.astype(BF16), vb) + w_inter * _dot(qb, ct.astype(BF16))
        den = jnp.sum(s_mat, axis=-1, keepdims=True) + w_inter * jnp.sum(q * n_prev, axis=-1, keepdims=True)
        hh = num / jnp.maximum(jnp.abs(den), jnp.exp(-m_t))

        b_last = b_col[L - 1:L, :]
        w_log = b_last - b_col + i_col
        m_new = jnp.maximum(b_last + m_prev, jnp.max(w_log, axis=0, keepdims=True))
        decay = jnp.exp(b_last + m_prev - m_new)
        kw = k * jnp.exp(w_log - m_new)
        ct_ref[h] = decay * ct + _dot_tn(kw.astype(BF16), vb)
        n_ref[h] = decay * n_prev + jnp.sum(kw, axis=0, keepdims=True)
        m_ref[h] = jnp.broadcast_to(m_new, m_ref.shape[1:])

        mu = jnp.mean(hh, axis=-1, keepdims=True)
        hc = hh - mu
        var = jnp.mean(hc * hc, axis=-1, keepdims=True)
        hn = hc * lax.rsqrt(var + LN_EPS) * ng_ref[:, h * ML_DV:(h + 1) * ML_DV]
        zlo = 2 * ML_QK + ML_INNER + h * ML_DV
        z = _dot(xb_of(c), win_ref[zlo // WCHUNK]) + bin_ref[:, zlo:zlo + ML_DV]
        hb_ref[c * L:(c + 1) * L, h * ML_DV:(h + 1) * ML_DV] = (hn * jax.nn.sigmoid(z)).astype(BF16)

    def out_half(c, half):
        return _dot(hb_ref[c * L:(c + 1) * L, :], wout_ref[half])

    def finish(c, halves):
        rows = slice(c * L, (c + 1) * L)
        y = jnp.concatenate(halves, axis=1)
        o_ref[rows, :] = _layer_norm(DN_ALPHA * x_ref[rows, :] + y, lng_ref[1:2, :], lnb_ref[1:2, :])

    n_pieces = 2 * ML_QK // WCHUNK
    for piece in range(n_pieces):
        qk_piece(0, piece)
    gate_vals = [gates(c) for c in range(n_chunks)]
    for c in range(n_chunks):
        halves = []
        for h in range(ML_HEADS):
            head(c, h, gate_vals[c])
            if c + 1 < n_chunks:
                qk_piece(c + 1, h)
            if c > 0 and h < 2:
                halves.append(out_half(c - 1, h))
                if h == 1:
                    finish(c - 1, halves)
    finish(n_chunks - 1, [out_half(n_chunks - 1, 0), out_half(n_chunks - 1, 1)])

    qk_ref[:, 0:HALO, :] = qk_ref[:, n_chunks * L:n_chunks * L + HALO, :]


def _mlstm_call(x, w_in, b_in, wg_row, bg_row, conv_w, norm_g, w_out, ln_g, ln_b, layer, j, cast_weights):
    S, D = x.shape
    L = ML_ROWS
    main_specs = [
        _resident((ML_MAIN // WCHUNK, D, WCHUNK), (0, 0, 0)),
        _resident((None, 1, ML_MAIN), (j, 0, 0)),
        _resident((None, 4 * ML_HEADS, D), (j, 0, 0)),
        _resident((None, 4 * ML_HEADS, 1), (j, 0, 0)),
        _resident((None, ML_CONV, 2 * ML_QK), (j, 0, 0)),
        _resident((None, 1, ML_INNER), (j, 0, 0)),
        _resident((D // WCHUNK, ML_INNER, WCHUNK), (0, 0, 0)),
        _resident((None, 4, D), (layer, 0, 0)),
        _resident((None, 4, D), (layer, 0, 0)),
    ]
    main_args = [w_in, b_in, wg_row, bg_row, conv_w, norm_g, w_out, ln_g, ln_b]
    scratch = [
        pltpu.VMEM((2 * ML_QK // V7X_LANES, V7X_SUBLANES + ML_TILE, V7X_LANES), F32),
        pltpu.VMEM((2 * ML_QK // V7X_LANES, ML_TILE, V7X_LANES), F32),
        pltpu.VMEM((ML_HEADS, ML_DQK, ML_DV), F32),
        pltpu.VMEM((ML_HEADS, 1, ML_DQK), F32),
        pltpu.VMEM((ML_HEADS, 1, 128), F32),
        pltpu.VMEM((ML_TILE, ML_INNER), BF16),
        pltpu.VMEM((ML_TILE, D), BF16),
    ]
    return _call(_mlstm_kernel, "mlstm", x, pl.BlockSpec((ML_TILE, D), lambda t: (t, 0)),
                 main_specs, main_args, cast_weights, scratch)


def _group_cols(w_ref, col0):
    off = col0 % WCHUNK
    return w_ref[col0 // WCHUNK, :, off:off + GM_GDIM]


def _gmlp_kernel(n_in, n_cast, *refs):
    ins, cast_in, o_ref, cast_out, (v_ref, u_ref, y_ref) = _split_refs(refs, n_in, n_cast)
    x_ref, win_ref, bin_ref, vng_ref, vnb_ref, ws_ref, bst_ref, wout_ref, lng_ref, lnb_ref = ins
    _run_casts(cast_in, cast_out)
    xb = x_ref[...].astype(BF16)
    for g in range(GM_GROUPS):
        lo = g * GM_GDIM
        vlo = GM_WIDTH + lo
        v_ref[:, lo:lo + GM_GDIM] = _gelu(_dot(xb, _group_cols(win_ref, vlo)) + bin_ref[:, vlo:vlo + GM_GDIM])
    for g in range(GM_GROUPS):
        lo = g * GM_GDIM
        u_ref[:, lo:lo + GM_GDIM] = _gelu(_dot(xb, _group_cols(win_ref, lo)) + bin_ref[:, lo:lo + GM_GDIM])
    v = v_ref[...]
    mu = jnp.mean(v, axis=-1, keepdims=True)
    vc = v - mu
    rstd = lax.rsqrt(jnp.mean(vc * vc, axis=-1, keepdims=True) + LN_EPS)

    blk_r = lax.broadcasted_iota(jnp.int32, (GM_BLOCK, GM_BLOCK), 0) // CHUNK
    blk_c = lax.broadcasted_iota(jnp.int32, (GM_BLOCK, GM_BLOCK), 1) // CHUNK
    chunk_causal = blk_r >= blk_c
    for g in range(GM_GROUPS):
        lo = g * GM_GDIM
        vn = ((v_ref[:, lo:lo + GM_GDIM] - mu) * rstd * vng_ref[:, lo:lo + GM_GDIM]
              + vnb_ref[:, lo:lo + GM_GDIM]).astype(BF16)
        wsm = jnp.where(chunk_causal, ws_ref[g], 0.0).astype(BF16)
        bias = bst_ref[:, g:g + 1]
        for nb in range(GM_ROWS // GM_BLOCK):
            r0 = nb * GM_BLOCK
            s = _dot(wsm, vn[r0:r0 + GM_BLOCK, :]) + bias
            y_ref[r0:r0 + GM_BLOCK, lo:lo + GM_GDIM] = (u_ref[r0:r0 + GM_BLOCK, lo:lo + GM_GDIM] * s).astype(BF16)
    for r in range(GM_ROWS // GM_SUB):
        rs = slice(r * GM_SUB, (r + 1) * GM_SUB)
        y = _dot_chunked(y_ref[rs, :], wout_ref)
        o_ref[rs, :] = _layer_norm(DN_ALPHA * x_ref[rs, :] + y, lng_ref[1:2, :], lnb_ref[1:2, :])


def _gmlp_call(x, w_in, b_in, vn_g, vn_b, ws, bs_t, w_out, ln_g, ln_b, layer, j, cast_weights):
    S, D = x.shape
    main_specs = [
        _resident((2 * GM_WIDTH // WCHUNK, D, WCHUNK), (0, 0, 0)),
        _resident((None, 1, 2 * GM_WIDTH), (j, 0, 0)),
        _resident((None, 1, GM_WIDTH), (j, 0, 0)),
        _resident((None, 1, GM_WIDTH), (j, 0, 0)),
        _resident((None, GM_GROUPS, GM_BLOCK, GM_BLOCK), (j, 0, 0, 0)),
        _resident((None, GM_BLOCK, GM_GROUPS), (j, 0, 0)),
        _resident((D // WCHUNK, GM_WIDTH, WCHUNK), (0, 0, 0)),
        _resident((None, 4, D), (layer, 0, 0)),
        _resident((None, 4, D), (layer, 0, 0)),
    ]
    main_args = [w_in, b_in, vn_g, vn_b, ws, bs_t, w_out, ln_g, ln_b]
    scratch = [
        pltpu.VMEM((GM_ROWS, GM_WIDTH), F32),
        pltpu.VMEM((GM_ROWS, GM_WIDTH), F32),
        pltpu.VMEM((GM_ROWS, GM_WIDTH), BF16),
    ]
    return _call(_gmlp_kernel, "gmlp", x, pl.BlockSpec((GM_ROWS, D), lambda t: (t, 0)),
                 main_specs, main_args, cast_weights, scratch)


def kernel(x, p, ln_g, ln_b, ffn1_wgu, ffn1_wd, ffn2_wgu, ffn2_wd, ml_w_in, ml_b_in, ml_conv, ml_norm_g, ml_w_out, gm_w_in, gm_b_in, gm_vn_g, gm_vn_b, gm_ws, gm_bs, gm_w_out, ple_wp, ple_wg, ple_bg):
    B, S, D = x.shape
    assert (B, S, D) == (1, SEQ, D_MODEL)
    xs = x.reshape(S, D)
    ps = p.reshape(DEPTH, S, PLE_DIM)

    ple_bg_r = ple_bg[:, None, :]
    ml_b_main = ml_b_in[:, None, :ML_MAIN]
    ml_wg_row = jnp.pad(jnp.swapaxes(ml_w_in[:, :, ML_MAIN:], 1, 2),
                        ((0, 0), (0, 2 * ML_HEADS), (0, 0))).astype(BF16)
    ml_bg_row = jnp.pad(ml_b_in[:, ML_MAIN:], ((0, 0), (0, 2 * ML_HEADS)))[:, :, None]
    ml_norm_g_r = ml_norm_g[:, None, :]
    ml_w_in_t = jnp.swapaxes(ml_w_in, 1, 2)
    gm_b_in_r = gm_b_in[:, None, :]
    gm_vn_g_r, gm_vn_b_r = gm_vn_g[:, None, :], gm_vn_b[:, None, :]
    gm_bs_t = jnp.swapaxes(gm_bs, 1, 2)

    wd0 = ffn1_wd[0].astype(BF16)
    ffn1_w = (ffn1_wgu[0].astype(BF16),
              jnp.stack([wd0[:, c * WCHUNK:(c + 1) * WCHUNK] for c in range(D_MODEL // WCHUNK)]))
    for i in range(DEPTH):
        j = i // 2
        if i % 2 == 0:
            mixer_f32 = [("transposed", ml_w_in_t, j), ("chunked", ml_w_out, j)]
        else:
            mixer_f32 = [("chunked", gm_w_in, j), ("chunked", gm_w_out, j)]
        ffn2_rest_f32 = [("chunked", ffn2_wd, i), ("chunked", ple_wg, i), ("plain", ple_wp, i)]
        xs, cast_w = _ffn_call(xs, *ffn1_w, ln_g, ln_b, i, 0, mixer_f32 + ffn2_rest_f32)
        mixer_w, ffn2_rest_w = cast_w[:2], cast_w[2:]
        ffn2_f32 = [("plain", ffn2_wgu, i)]
        if i % 2 == 0:
            xs, ffn2_w = _mlstm_call(xs, mixer_w[0], ml_b_main, ml_wg_row, ml_bg_row,
                                     ml_conv, ml_norm_g_r, mixer_w[1], ln_g, ln_b, i, j, ffn2_f32)
        else:
            xs, ffn2_w = _gmlp_call(xs, mixer_w[0], gm_b_in_r, gm_vn_g_r, gm_vn_b_r, gm_ws, gm_bs_t,
                                    mixer_w[1], ln_g, ln_b, i, j, ffn2_f32)
        next_f32 = [("plain", ffn1_wgu, i + 1), ("chunked", ffn1_wd, i + 1)] if i + 1 < DEPTH else []
        xs, ffn1_w = _ffn_call(xs, ffn2_w[0], ffn2_rest_w[0], ln_g, ln_b, i, 2, next_f32,
                               ple=(ps, ffn2_rest_w[1], ple_bg_r, ffn2_rest_w[2]))
    return xs.reshape(B, S, D)
```

```python
import functools

import jax
import jax.numpy as jnp
from jax import lax
from jax.experimental import pallas as pl
from jax.experimental.pallas import tpu as pltpu

D_MODEL = 1024
SEQ = 16384
DEPTH = 4
CHUNK = 64
PLE_DIM = 256
FFN_HIDDEN = 2816
ML_HEADS = 4
ML_DQK = 256
ML_DV = 512
ML_INNER = ML_HEADS * ML_DV
ML_QK = ML_HEADS * ML_DQK
ML_MAIN = 2 * ML_QK + 2 * ML_INNER
ML_CONV = 4
GM_BLOCK = 128
GM_GROUPS = 8
GM_WIDTH = 2 * D_MODEL
GM_GDIM = GM_WIDTH // GM_GROUPS
DN_ALPHA = (2.0 * DEPTH) ** 0.25
LN_EPS = 1e-5

V7X_MXU_DIM = 256
V7X_SUBLANES = 8
V7X_LANES = 128
V7X_BF16_ROWS = 16
V7X_VMEM_BYTES = 64 * 1024 * 1024
VMEM_LIMIT_BYTES = V7X_VMEM_BYTES * 7 // 8

FFN_ROWS = 1024
FFN_PLE_ROWS = 1024
FFN_COLS = V7X_MXU_DIM
FFN_SUB = 256
ML_ROWS = 256
ML_TILE = 512
GM_ROWS = 1024
GM_SUB = 256
WCHUNK = 512

F32 = jnp.float32
BF16 = jnp.bfloat16


def _dot(a, b):
    return jnp.dot(a, b, preferred_element_type=F32)


def _dot_nt(a, b):
    return lax.dot_general(a, b, (((1,), (1,)), ((), ())), preferred_element_type=F32)


def _dot_tn(a, b):
    return lax.dot_general(a, b, (((0,), (0,)), ((), ())), preferred_element_type=F32)


def _dot_chunked(a, w_ref):
    return jnp.concatenate([_dot(a, w_ref[c]) for c in range(w_ref.shape[0])], axis=1)


def _layer_norm(z, g, b):
    mu = jnp.mean(z, axis=-1, keepdims=True)
    zc = z - mu
    var = jnp.mean(zc * zc, axis=-1, keepdims=True)
    return zc * lax.rsqrt(var + LN_EPS) * g + b


def _split3(a):
    hi = a.astype(BF16)
    r1 = a - hi.astype(F32)
    mid = r1.astype(BF16)
    lo = (r1 - mid.astype(F32)).astype(BF16)
    return hi, mid, lo


def _gelu(x):
    return 0.5 * x * (1.0 + lax.erf(x * (2.0 ** -0.5)))


def _log_sigmoid(x):
    return jnp.minimum(x, 0.0) - jnp.log1p(jnp.exp(-jnp.abs(x)))


def _resident(shape, index):
    return pl.BlockSpec(shape, lambda t: index, pipeline_mode=pl.Buffered(1))


def _cast_plan(weights, steps):
    in_specs, out_specs, out_shapes, args = [], [], [], []
    for kind, w, layer in weights:
        args.append(w)
        if kind == "transposed":
            _, N, K = w.shape
            last = N // WCHUNK - 1
            assert steps > last
            in_specs.append(pl.BlockSpec((None, WCHUNK, K),
                                         lambda t, layer=layer, last=last: (layer, jnp.minimum(t, last), 0)))
            out_specs.append(pl.BlockSpec((None, K, WCHUNK), lambda t, last=last: (jnp.minimum(t, last), 0, 0)))
            out_shapes.append(jax.ShapeDtypeStruct((last + 1, K, WCHUNK), BF16))
            continue
        _, K, N = w.shape
        hold = 1
        while K % (steps // hold) or (K // (steps // hold)) % V7X_BF16_ROWS:
            hold *= 2
        rows = K // (steps // hold)
        in_specs.append(pl.BlockSpec((None, rows, N), lambda t, layer=layer, hold=hold: (layer, t // hold, 0)))
        if kind == "chunked":
            n = N // WCHUNK
            out_specs.append(pl.BlockSpec((n, rows, WCHUNK), lambda t, hold=hold: (0, t // hold, 0)))
            out_shapes.append(jax.ShapeDtypeStruct((n, K, WCHUNK), BF16))
        else:
            out_specs.append(pl.BlockSpec((rows, N), lambda t, hold=hold: (t // hold, 0)))
            out_shapes.append(jax.ShapeDtypeStruct((K, N), BF16))
    return in_specs, out_specs, out_shapes, args


def _split_refs(refs, n_in, n_cast):
    a, b, c = n_in, n_in + n_cast, n_in + 2 * n_cast + 1
    return refs[:a], refs[a:b], refs[b], refs[b + 1:c], refs[c:]


def _run_casts(cast_in, cast_out):
    for src, dst in zip(cast_in, cast_out):
        if len(dst.shape) == 3:
            width = dst.shape[2]
            for c in range(dst.shape[0]):
                dst[c] = src[:, c * width:(c + 1) * width].astype(BF16)
        elif dst.shape != src.shape:
            dst[...] = src[...].T.astype(BF16)
        else:
            dst[...] = src[...].astype(BF16)


def _call(kernel_fn, name, x, row_spec, main_specs, main_args, cast_weights, scratch_shapes):
    S, D = x.shape
    steps = S // row_spec.block_shape[0]
    c_in, c_out, c_shapes, c_args = _cast_plan(cast_weights, steps)
    outs = pl.pallas_call(
        functools.partial(kernel_fn, 1 + len(main_args), len(c_args)),
        grid=(steps,),
        in_specs=[row_spec] + main_specs + c_in,
        out_specs=[row_spec] + c_out,
        out_shape=[jax.ShapeDtypeStruct((S, D), F32)] + c_shapes,
        scratch_shapes=scratch_shapes,
        compiler_params=pltpu.CompilerParams(
            dimension_semantics=("arbitrary",), vmem_limit_bytes=VMEM_LIMIT_BYTES),
        name=name,
    )(x, *main_args, *c_args)
    return outs[0], outs[1:]


def _ffn_kernel(with_ple, ln_row, n_in, n_cast, *refs):
    ins, cast_in, o_ref, cast_out, (a_ref, xb_ref) = _split_refs(refs, n_in, n_cast)
    if with_ple:
        x_ref, wgu_ref, wd_ref, lng_ref, lnb_ref, p_ref, wpg_ref, bpg_ref, wpp_ref = ins
    else:
        x_ref, wgu_ref, wd_ref, lng_ref, lnb_ref = ins
    _run_casts(cast_in, cast_out)
    xb_ref[...] = x_ref[...].astype(BF16)
    for c in range(FFN_HIDDEN // FFN_COLS):
        lo = c * FFN_COLS
        g = _dot(xb_ref[...], wgu_ref[:, lo:lo + FFN_COLS])
        u = _dot(xb_ref[...], wgu_ref[:, FFN_HIDDEN + lo:FFN_HIDDEN + lo + FFN_COLS])
        a_ref[:, lo:lo + FFN_COLS] = (jax.nn.silu(g) * u).astype(BF16)
    n_sub = a_ref.shape[0] // FFN_SUB

    def down(r):
        rs = slice(r * FFN_SUB, (r + 1) * FFN_SUB)
        y = _dot_chunked(a_ref[rs, :], wd_ref)
        return _layer_norm(DN_ALPHA * x_ref[rs, :] + 0.5 * y,
                           lng_ref[ln_row:ln_row + 1, :], lnb_ref[ln_row:ln_row + 1, :])

    x1 = down(0)
    for r in range(n_sub):
        rs = slice(r * FFN_SUB, (r + 1) * FFN_SUB)
        x1_next = down(r + 1) if r + 1 < n_sub else None
        if with_ple:
            gate = jax.nn.sigmoid(_dot_chunked(x1.astype(BF16), wpg_ref) + bpg_ref[...])
            emb = _dot(p_ref[rs, :].astype(BF16), wpp_ref[...])
            x1 = _layer_norm(DN_ALPHA * x1 + gate * emb, lng_ref[3:4, :], lnb_ref[3:4, :])
        o_ref[rs, :] = x1
        x1 = x1_next


def _ffn_call(x, wgu, wd, ln_g, ln_b, layer, ln_row, cast_weights, ple=None):
    S, D = x.shape
    F = FFN_HIDDEN
    tm = FFN_ROWS if ple is None else FFN_PLE_ROWS
    main_specs = [
        _resident((D, 2 * F), (0, 0)),
        _resident((D // WCHUNK, F, WCHUNK), (0, 0, 0)),
        _resident((None, 4, D), (layer, 0, 0)),
        _resident((None, 4, D), (layer, 0, 0)),
    ]
    main_args = [wgu, wd, ln_g, ln_b]
    if ple is not None:
        p, wpg, bpg, wpp = ple
        main_specs += [
            pl.BlockSpec((None, tm, PLE_DIM), lambda t: (layer, t, 0)),
            _resident((D // WCHUNK, D, WCHUNK), (0, 0, 0)),
            _resident((None, 1, D), (layer, 0, 0)),
            _resident((PLE_DIM, D), (0, 0)),
        ]
        main_args += [p, wpg, bpg, wpp]
    return _call(functools.partial(_ffn_kernel, ple is not None, ln_row),
                 "ffn_ple" if ple is not None else "ffn",
                 x, pl.BlockSpec((tm, D), lambda t: (t, 0)), main_specs, main_args, cast_weights,
                 [pltpu.VMEM((tm, F), BF16), pltpu.VMEM((tm, D), BF16)])


def _conv_silu(qk_ref, act_ref, conv_ref, row0, col0, scale):
    L = ML_ROWS
    HALO = V7X_SUBLANES
    groups = L // V7X_SUBLANES
    outs = []
    for b in range(ML_DQK // V7X_LANES):
        blk = col0 // V7X_LANES + b
        acc = None
        for j in range(ML_CONV):
            start = row0 + HALO - (ML_CONV - 1) + j
            tap = jnp.concatenate(
                [qk_ref[blk, pl.ds(start + s, groups, stride=V7X_SUBLANES), :] for s in range(V7X_SUBLANES)], axis=0)
            term = conv_ref[j:j + 1, blk * V7X_LANES:(blk + 1) * V7X_LANES] * tap
            acc = term if acc is None else acc + term
        act = jax.nn.silu(acc) * scale if scale != 1.0 else jax.nn.silu(acc)
        for s in range(V7X_SUBLANES):
            act_ref[blk, pl.ds(row0 + s, groups, stride=V7X_SUBLANES), :] = act[s * groups:(s + 1) * groups, :]
        outs.append(act_ref[blk, row0:row0 + L, :])
    return jnp.concatenate(outs, axis=1)


def _mlstm_kernel(n_in, n_cast, *refs):
    ins, cast_in, o_ref, cast_out, scratch = _split_refs(refs, n_in, n_cast)
    (x_ref, win_ref, bin_ref, wgr_ref, bgr_ref, conv_ref, ng_ref,
     wout_ref, lng_ref, lnb_ref) = ins
    qk_ref, act_ref, ct_ref, n_ref, m_ref, hb_ref, xb_ref = scratch
    L = ML_ROWS
    HALO = V7X_SUBLANES
    n_chunks = x_ref.shape[0] // L
    n_blk = WCHUNK // V7X_LANES

    @pl.when(pl.program_id(0) == 0)
    def _init():
        qk_ref[:, 0:HALO, :] = jnp.zeros((2 * ML_QK // V7X_LANES, HALO, V7X_LANES), F32)
        ct_ref[...] = jnp.zeros_like(ct_ref)
        n_ref[...] = jnp.zeros_like(n_ref)
        m_ref[...] = jnp.zeros_like(m_ref)

    _run_casts(cast_in, cast_out)
    xb_ref[...] = x_ref[...].astype(BF16)

    def xb_of(c):
        return xb_ref[c * L:(c + 1) * L, :]

    ri = lax.broadcasted_iota(jnp.int32, (L, L), 0)
    ci = lax.broadcasted_iota(jnp.int32, (L, L), 1)
    causal = ri >= ci
    upper = (ri <= ci).astype(BF16)

    def qk_piece(c, piece):
        pre = _dot(xb_of(c), win_ref[piece]) + bin_ref[:, piece * WCHUNK:(piece + 1) * WCHUNK]
        for b in range(n_blk):
            qk_ref[piece * n_blk + b, HALO + c * L:HALO + (c + 1) * L, :] = pre[:, b * V7X_LANES:(b + 1) * V7X_LANES]

    def gates(c):
        g_row = _dot_nt(wgr_ref[...], xb_of(c)) + bgr_ref[...]
        n_rows = g_row.shape[0]
        parts = _dot(jnp.concatenate(_split3(_log_sigmoid(g_row)), axis=0), upper)
        bcum_row = parts[0:n_rows] + parts[n_rows:2 * n_rows] + parts[2 * n_rows:]
        pad = jnp.zeros((V7X_LANES - 2 * n_rows, L), F32)
        cols = jnp.concatenate([g_row, bcum_row, pad], axis=0).T
        return cols[:, 0:n_rows], g_row, cols[:, n_rows:2 * n_rows], bcum_row

    def head(c, h, gate_vals):
        g_col, g_row, bcum_col, bcum_row = gate_vals
        q = _conv_silu(qk_ref, act_ref, conv_ref, c * L, h * ML_DQK, 1.0)
        k = _conv_silu(qk_ref, act_ref, conv_ref, c * L, ML_QK + h * ML_DQK, ML_DQK ** -0.5)
        qb = q.astype(BF16)
        kb = k.astype(BF16)
        vlo = 2 * ML_QK + h * ML_DV
        v = _dot(xb_of(c), win_ref[vlo // WCHUNK]) + bin_ref[:, vlo:vlo + ML_DV]
        vb = v.astype(BF16)

        i_col = g_col[:, h:h + 1]
        b_col = bcum_col[:, ML_HEADS + h:ML_HEADS + h + 1]
        i_row = g_row[h:h + 1, :]
        b_row = bcum_row[ML_HEADS + h:ML_HEADS + h + 1, :]
        m_prev = m_ref[h][:, 0:1]
        ct = ct_ref[h]
        n_prev = n_ref[h]

        d_log = jnp.where(causal, b_col - b_row + i_row, -jnp.inf)
        inter = b_col + m_prev
        m_t = jnp.maximum(inter, jnp.max(d_log, axis=-1, keepdims=True))
        s_mat = _dot_nt(qb, kb) * jnp.exp(d_log - m_t)
        w_inter = jnp.exp(inter - m_t)
        num = _dot(s_mat.astype(BF16), vb) + w_inter * _dot(qb, ct.astype(BF16))
        den = jnp.sum(s_mat, axis=-1, keepdims=True) + w_inter * jnp.sum(q * n_prev, axis=-1, keepdims=True)
        hh = num / jnp.maximum(jnp.abs(den), jnp.exp(-m_t))

        b_last = b_col[L - 1:L, :]
        w_log = b_last - b_col + i_col
        m_new = jnp.maximum(b_last + m_prev, jnp.max(w_log, axis=0, keepdims=True))
        decay = jnp.exp(b_last + m_prev - m_new)
        kw = k * jnp.exp(w_log - m_new)
        ct_ref[h] = decay * ct + _dot_tn(kw.astype(BF16), vb)
        n_ref[h] = decay * n_prev + jnp.sum(kw, axis=0, keepdims=True)
        m_ref[h] = jnp.broadcast_to(m_new, m_ref.shape[1:])

        mu = jnp.mean(hh, axis=-1, keepdims=True)
        hc = hh - mu
        var = jnp.mean(hc * hc, axis=-1, keepdims=True)
        hn = hc * lax.rsqrt(var + LN_EPS) * ng_ref[:, h * ML_DV:(h + 1) * ML_DV]
        zlo = 2 * ML_QK + ML_INNER + h * ML_DV
        z = _dot(xb_of(c), win_ref[zlo // WCHUNK]) + bin_ref[:, zlo:zlo + ML_DV]
        hb_ref[c * L:(c + 1) * L, h * ML_DV:(h + 1) * ML_DV] = (hn * jax.nn.sigmoid(z)).astype(BF16)

    def out_half(c, half):
        return _dot(hb_ref[c * L:(c + 1) * L, :], wout_ref[half])

    def finish(c, halves):
        rows = slice(c * L, (c + 1) * L)
        y = jnp.concatenate(halves, axis=1)
        o_ref[rows, :] = _layer_norm(DN_ALPHA * x_ref[rows, :] + y, lng_ref[1:2, :], lnb_ref[1:2, :])

    n_pieces = 2 * ML_QK // WCHUNK
    for piece in range(n_pieces):
        qk_piece(0, piece)
    gate_vals = [gates(c) for c in range(n_chunks)]
    for c in range(n_chunks):
        halves = []
        for h in range(ML_HEADS):
            head(c, h, gate_vals[c])
            if c + 1 < n_chunks:
                qk_piece(c + 1, h)
            if c > 0 and h < 2:
                halves.append(out_half(c - 1, h))
                if h == 1:
                    finish(c - 1, halves)
    finish(n_chunks - 1, [out_half(n_chunks - 1, 0), out_half(n_chunks - 1, 1)])

    qk_ref[:, 0:HALO, :] = qk_ref[:, n_chunks * L:n_chunks * L + HALO, :]


def _mlstm_call(x, w_in, b_in, wg_row, bg_row, conv_w, norm_g, w_out, ln_g, ln_b, layer, j, cast_weights):
    S, D = x.shape
    L = ML_ROWS
    main_specs = [
        _resident((ML_MAIN // WCHUNK, D, WCHUNK), (0, 0, 0)),
        _resident((None, 1, ML_MAIN), (j, 0, 0)),
        _resident((None, 4 * ML_HEADS, D), (j, 0, 0)),
        _resident((None, 4 * ML_HEADS, 1), (j, 0, 0)),
        _resident((None, ML_CONV, 2 * ML_QK), (j, 0, 0)),
        _resident((None, 1, ML_INNER), (j, 0, 0)),
        _resident((D // WCHUNK, ML_INNER, WCHUNK), (0, 0, 0)),
        _resident((None, 4, D), (layer, 0, 0)),
        _resident((None, 4, D), (layer, 0, 0)),
    ]
    main_args = [w_in, b_in, wg_row, bg_row, conv_w, norm_g, w_out, ln_g, ln_b]
    scratch = [
        pltpu.VMEM((2 * ML_QK // V7X_LANES, V7X_SUBLANES + ML_TILE, V7X_LANES), F32),
        pltpu.VMEM((2 * ML_QK // V7X_LANES, ML_TILE, V7X_LANES), F32),
        pltpu.VMEM((ML_HEADS, ML_DQK, ML_DV), F32),
        pltpu.VMEM((ML_HEADS, 1, ML_DQK), F32),
        pltpu.VMEM((ML_HEADS, 1, 128), F32),
        pltpu.VMEM((ML_TILE, ML_INNER), BF16),
        pltpu.VMEM((ML_TILE, D), BF16),
    ]
    return _call(_mlstm_kernel, "mlstm", x, pl.BlockSpec((ML_TILE, D), lambda t: (t, 0)),
                 main_specs, main_args, cast_weights, scratch)


def _group_cols(w_ref, col0):
    off = col0 % WCHUNK
    return w_ref[col0 // WCHUNK, :, off:off + GM_GDIM]


def _gmlp_kernel(n_in, n_cast, *refs):
    ins, cast_in, o_ref, cast_out, (v_ref, u_ref, y_ref) = _split_refs(refs, n_in, n_cast)
    x_ref, win_ref, bin_ref, vng_ref, vnb_ref, ws_ref, bst_ref, wout_ref, lng_ref, lnb_ref = ins
    _run_casts(cast_in, cast_out)
    xb = x_ref[...].astype(BF16)
    for g in range(GM_GROUPS):
        lo = g * GM_GDIM
        vlo = GM_WIDTH + lo
        v_ref[:, lo:lo + GM_GDIM] = _gelu(_dot(xb, _group_cols(win_ref, vlo)) + bin_ref[:, vlo:vlo + GM_GDIM])
    for g in range(GM_GROUPS):
        lo = g * GM_GDIM
        u_ref[:, lo:lo + GM_GDIM] = _gelu(_dot(xb, _group_cols(win_ref, lo)) + bin_ref[:, lo:lo + GM_GDIM])
    v = v_ref[...]
    mu = jnp.mean(v, axis=-1, keepdims=True)
    vc = v - mu
    rstd = lax.rsqrt(jnp.mean(vc * vc, axis=-1, keepdims=True) + LN_EPS)

    blk_r = lax.broadcasted_iota(jnp.int32, (GM_BLOCK, GM_BLOCK), 0) // CHUNK
    blk_c = lax.broadcasted_iota(jnp.int32, (GM_BLOCK, GM_BLOCK), 1) // CHUNK
    chunk_causal = blk_r >= blk_c
    for g in range(GM_GROUPS):
        lo = g * GM_GDIM
        vn = ((v_ref[:, lo:lo + GM_GDIM] - mu) * rstd * vng_ref[:, lo:lo + GM_GDIM]
              + vnb_ref[:, lo:lo + GM_GDIM]).astype(BF16)
        wsm = jnp.where(chunk_causal, ws_ref[g], 0.0).astype(BF16)
        bias = bst_ref[:, g:g + 1]
        for nb in range(GM_ROWS // GM_BLOCK):
            r0 = nb * GM_BLOCK
            s = _dot(wsm, vn[r0:r0 + GM_BLOCK, :]) + bias
            y_ref[r0:r0 + GM_BLOCK, lo:lo + GM_GDIM] = (u_ref[r0:r0 + GM_BLOCK, lo:lo + GM_GDIM] * s).astype(BF16)
    for r in range(GM_ROWS // GM_SUB):
        rs = slice(r * GM_SUB, (r + 1) * GM_SUB)
        y = _dot_chunked(y_ref[rs, :], wout_ref)
        o_ref[rs, :] = _layer_norm(DN_ALPHA * x_ref[rs, :] + y, lng_ref[1:2, :], lnb_ref[1:2, :])


def _gmlp_call(x, w_in, b_in, vn_g, vn_b, ws, bs_t, w_out, ln_g, ln_b, layer, j, cast_weights):
    S, D = x.shape
    main_specs = [
        _resident((2 * GM_WIDTH // WCHUNK, D, WCHUNK), (0, 0, 0)),
        _resident((None, 1, 2 * GM_WIDTH), (j, 0, 0)),
        _resident((None, 1, GM_WIDTH), (j, 0, 0)),
        _resident((None, 1, GM_WIDTH), (j, 0, 0)),
        _resident((None, GM_GROUPS, GM_BLOCK, GM_BLOCK), (j, 0, 0, 0)),
        _resident((None, GM_BLOCK, GM_GROUPS), (j, 0, 0)),
        _resident((D // WCHUNK, GM_WIDTH, WCHUNK), (0, 0, 0)),
        _resident((None, 4, D), (layer, 0, 0)),
        _resident((None, 4, D), (layer, 0, 0)),
    ]
    main_args = [w_in, b_in, vn_g, vn_b, ws, bs_t, w_out, ln_g, ln_b]
    scratch = [
        pltpu.VMEM((GM_ROWS, GM_WIDTH), F32),
        pltpu.VMEM((GM_ROWS, GM_WIDTH), F32),
        pltpu.VMEM((GM_ROWS, GM_WIDTH), BF16),
    ]
    return _call(_gmlp_kernel, "gmlp", x, pl.BlockSpec((GM_ROWS, D), lambda t: (t, 0)),
                 main_specs, main_args, cast_weights, scratch)


def kernel(x, p, ln_g, ln_b, ffn1_wgu, ffn1_wd, ffn2_wgu, ffn2_wd, ml_w_in, ml_b_in, ml_conv, ml_norm_g, ml_w_out, gm_w_in, gm_b_in, gm_vn_g, gm_vn_b, gm_ws, gm_bs, gm_w_out, ple_wp, ple_wg, ple_bg):
    B, S, D = x.shape
    assert (B, S, D) == (1, SEQ, D_MODEL)
    xs = x.reshape(S, D)
    ps = p.reshape(DEPTH, S, PLE_DIM)

    ple_bg_r = ple_bg[:, None, :]
    ml_b_main = ml_b_in[:, None, :ML_MAIN]
    ml_wg_row = jnp.pad(jnp.swapaxes(ml_w_in[:, :, ML_MAIN:], 1, 2),
                        ((0, 0), (0, 2 * ML_HEADS), (0, 0))).astype(BF16)
    ml_bg_row = jnp.pad(ml_b_in[:, ML_MAIN:], ((0, 0), (0, 2 * ML_HEADS)))[:, :, None]
    ml_norm_g_r = ml_norm_g[:, None, :]
    ml_w_in_t = jnp.swapaxes(ml_w_in, 1, 2)
    gm_b_in_r = gm_b_in[:, None, :]
    gm_vn_g_r, gm_vn_b_r = gm_vn_g[:, None, :], gm_vn_b[:, None, :]
    gm_bs_t = jnp.swapaxes(gm_bs, 1, 2)

    wd0 = ffn1_wd[0].astype(BF16)
    ffn1_w = (ffn1_wgu[0].astype(BF16),
              jnp.stack([wd0[:, c * WCHUNK:(c + 1) * WCHUNK] for c in range(D_MODEL // WCHUNK)]))
    for i in range(DEPTH):
        j = i // 2
        if i % 2 == 0:
            mixer_f32 = [("transposed", ml_w_in_t, j), ("chunked", ml_w_out, j)]
        else:
            mixer_f32 = [("chunked", gm_w_in, j), ("chunked", gm_w_out, j)]
        ffn2_rest_f32 = [("chunked", ffn2_wd, i), ("chunked", ple_wg, i), ("plain", ple_wp, i)]
        xs, cast_w = _ffn_call(xs, *ffn1_w, ln_g, ln_b, i, 0, mixer_f32 + ffn2_rest_f32)
        mixer_w, ffn2_rest_w = cast_w[:2], cast_w[2:]
        ffn2_f32 = [("plain", ffn2_wgu, i)]
        if i % 2 == 0:
            xs, ffn2_w = _mlstm_call(xs, mixer_w[0], ml_b_main, ml_wg_row, ml_bg_row,
                                     ml_conv, ml_norm_g_r, mixer_w[1], ln_g, ln_b, i, j, ffn2_f32)
        else:
            xs, ffn2_w = _gmlp_call(xs, mixer_w[0], gm_b_in_r, gm_vn_g_r, gm_vn_b_r, gm_ws, gm_bs_t,
                                    mixer_w[1], ln_g, ln_b, i, j, ffn2_f32)
        next_f32 = [("plain", ffn1_wgu, i + 1), ("chunked", ffn1_wd, i + 1)] if i + 1 < DEPTH else []
        xs, ffn1_w = _ffn_call(xs, ffn2_w[0], ffn2_rest_w[0], ln_g, ln_b, i, 2, next_f32,
                               ple=(ps, ffn2_rest_w[1], ple_bg_r, ffn2_rest_w[2]))
    return xs.reshape(B, S, D)
```
